```python
import math
import jax, jax.numpy as jnp
from jax import lax
import numpy as np

D_MODEL = 1024
BATCH = 2
SEQ = 16384
DEPTH = 2

N_MIXERS = 4
GROUP_W = D_MODEL // N_MIXERS
D_MIX = N_MIXERS * GROUP_W

MLA_HEADS = 4
MLA_V_DIM = GROUP_W // MLA_HEADS
MLA_NOPE_DIM = MLA_V_DIM // 2
MLA_ROPE_DIM = MLA_V_DIM // 4
MLA_Q_LORA = GROUP_W
MLA_KV_LORA = GROUP_W // 2
ATTN_BLOCK = 128

SSD_HEADS = 4
SSD_HEAD_DIM = GROUP_W // SSD_HEADS
SSD_INNER = SSD_HEADS * SSD_HEAD_DIM
SSD_GROUPS = 2
SSD_STATE = 128
SSD_XBC = SSD_INNER + 2 * SSD_GROUPS * SSD_STATE
SSD_CONV = 4
SSD_CHUNK = 128

RET_HEADS = 4
RET_HEAD_DIM = GROUP_W // RET_HEADS
RET_CHUNK = 128

LRU_WIDTH = GROUP_W
LRU_BLOCKS = 4
LRU_BLOCK_DIM = LRU_WIDTH // LRU_BLOCKS
LRU_CONV = 4
LRU_C = 8.0

D_FF = ((8 * D_MODEL + 3 * 256 - 1) // (3 * 256)) * 256

ROPE_THETA = 10000.0
NORM_EPS = 1e-5
ALPHA = (2.0 * DEPTH) ** 0.25
BETA = (8.0 * DEPTH) ** -0.25

N_IN = (MLA_Q_LORA + MLA_KV_LORA + MLA_ROPE_DIM + SSD_INNER + SSD_XBC + SSD_HEADS
        + 4 * GROUP_W + 2 * LRU_WIDTH)

F32 = jnp.float32

kernel_name = "hybrid_mla_ssd_retention_rglru_block"


def _in_splits():
    widths = [MLA_Q_LORA, MLA_KV_LORA, MLA_ROPE_DIM,
              SSD_INNER, SSD_XBC, SSD_HEADS,
              GROUP_W, GROUP_W, GROUP_W, GROUP_W,
              LRU_WIDTH, LRU_WIDTH]
    return [int(v) for v in np.cumsum(widths)[:-1]]


def layernorm(x, g, b):
    xf = x.astype(F32)
    mu = jnp.mean(xf, -1, keepdims=True)
    var = jnp.mean(jnp.square(xf - mu), -1, keepdims=True)
    return ((xf - mu) * lax.rsqrt(var + NORM_EPS) * g + b).astype(x.dtype)


def rmsnorm(x, g):
    xf = x.astype(F32)
    return (xf * lax.rsqrt(jnp.mean(jnp.square(xf), -1, keepdims=True) + NORM_EPS) * g).astype(x.dtype)


def rope(x, positions):
    d = x.shape[-1]
    inv = ROPE_THETA ** (-jnp.arange(0, d, 2, dtype=F32) / d)
    ang = positions.astype(F32)[..., None] * inv
    cos = jnp.cos(ang)[:, :, None, :].astype(x.dtype)
    sin = jnp.sin(ang)[:, :, None, :].astype(x.dtype)
    x1, x2 = x[..., : d // 2], x[..., d // 2:]
    return jnp.concatenate([x1 * cos - x2 * sin, x1 * sin + x2 * cos], -1)


def causal_dwconv(x, w, b):
    k, c = w.shape
    y = lax.conv_general_dilated(x, w[:, None, :], window_strides=(1,), padding=[(k - 1, 0)],
                                 dimension_numbers=("NWC", "WIO", "NWC"), feature_group_count=c)
    return y + b


def mla_mixer(cq, ckv, kr, positions, g_q, w_uq, g_kv, w_ukv):
    b, s, _ = cq.shape
    dqk = MLA_NOPE_DIM + MLA_ROPE_DIM
    q = (rmsnorm(cq, g_q) @ w_uq).reshape(b, s, MLA_HEADS, dqk)
    q = jnp.concatenate([q[..., :MLA_NOPE_DIM], rope(q[..., MLA_NOPE_DIM:], positions)], -1)
    kv = (rmsnorm(ckv, g_kv) @ w_ukv).reshape(b, s, MLA_HEADS, MLA_NOPE_DIM + MLA_V_DIM)
    k_rope = jnp.broadcast_to(rope(kr[:, :, None, :], positions), (b, s, MLA_HEADS, MLA_ROPE_DIM))
    k = jnp.concatenate([kv[..., :MLA_NOPE_DIM], k_rope], -1)
    v = kv[..., MLA_NOPE_DIM:]
    scale = dqk ** -0.5
    nb = s // ATTN_BLOCK
    q_blocks = q.reshape(b, nb, ATTN_BLOCK, MLA_HEADS, dqk).transpose(1, 0, 2, 3, 4)
    k_pos = jnp.arange(s)

    def block(args):
        qb, i = args
        q_pos = i * ATTN_BLOCK + jnp.arange(ATTN_BLOCK)
        sc = jnp.einsum("bqhd,bkhd->bhqk", qb, k, preferred_element_type=F32) * scale
        sc = jnp.where(k_pos[None, :] <= q_pos[:, None], sc, -jnp.inf)
        p = jax.nn.softmax(sc, axis=-1).astype(v.dtype)
        return jnp.einsum("bhqk,bkhd->bqhd", p, v)

    o = lax.map(block, (q_blocks, jnp.arange(nb)))
    return o.transpose(1, 0, 2, 3, 4).reshape(b, s, MLA_HEADS * MLA_V_DIM)


def ssd_chunked(x, a, bm, cm):
    b, s, h, p = x.shape
    n = bm.shape[-1]
    c, l = s // SSD_CHUNK, SSD_CHUNK
    x = x.reshape(b, c, l, h, p)
    bm = bm.reshape(b, c, l, h, n)
    cm = cm.reshape(b, c, l, h, n)
    a_cs = jnp.cumsum(a.reshape(b, c, l, h).transpose(0, 3, 1, 2), -1)
    tril = jnp.tril(jnp.ones((l, l), dtype=bool))
    seg = jnp.exp(jnp.where(tril, a_cs[..., :, None] - a_cs[..., None, :], -jnp.inf))
    y_diag = jnp.einsum("bclhn,bcshn,bhcls,bcshp->bclhp", cm, bm, seg, x)
    decay_to_end = jnp.exp(a_cs[..., -1:] - a_cs)
    chunk_states = jnp.einsum("bclhn,bhcl,bclhp->bchpn", bm, decay_to_end, x)
    chunk_decay = jnp.exp(a_cs[..., -1])

    def step(state, inp):
        st, dec = inp
        return state * dec[..., None, None] + st, state

    _, prev = lax.scan(step, jnp.zeros((b, h, p, n), F32),
                       (chunk_states.transpose(1, 0, 2, 3, 4), chunk_decay.transpose(2, 0, 1)))
    prev = prev.transpose(1, 0, 2, 3, 4)
    y_off = jnp.einsum("bclhn,bchpn,bhcl->bclhp", cm, prev, jnp.exp(a_cs))
    return (y_diag + y_off).reshape(b, s, h, p)


def ssd_mixer(z, xbc, dt_raw, conv_w, conv_b, dt_bias, a_log, d_skip, norm_g):
    b, s, _ = z.shape
    xbc = jax.nn.silu(causal_dwconv(xbc, conv_w, conv_b)).astype(F32)
    gn = SSD_GROUPS * SSD_STATE
    rep = SSD_HEADS // SSD_GROUPS
    xs = xbc[..., :SSD_INNER].reshape(b, s, SSD_HEADS, SSD_HEAD_DIM)
    bm = jnp.repeat(xbc[..., SSD_INNER:SSD_INNER + gn].reshape(b, s, SSD_GROUPS, SSD_STATE), rep, axis=2)
    cm = jnp.repeat(xbc[..., SSD_INNER + gn:].reshape(b, s, SSD_GROUPS, SSD_STATE), rep, axis=2)
    dt = jax.nn.softplus(dt_raw.astype(F32) + dt_bias.astype(F32))
    a = -jnp.exp(a_log.astype(F32))
    y = ssd_chunked(xs * dt[..., None], dt * a, bm, cm) + xs * d_skip.astype(F32)[:, None]
    y = y.reshape(b, s, SSD_INNER) * jax.nn.silu(z.astype(F32))
    return rmsnorm(y, norm_g.astype(F32)).astype(z.dtype)


def retention_mixer(q, k, v, g, positions, gn_g, gn_b):
    b, s, _ = q.shape
    H, d, l = RET_HEADS, RET_HEAD_DIM, RET_CHUNK
    c = s // l
    q = rope(q.reshape(b, s, H, d), positions).astype(F32)
    k = rope(k.reshape(b, s, H, d), positions).astype(F32) * (d ** -0.5)
    v = v.reshape(b, s, H, d).astype(F32)
    log_gamma = jnp.log1p(-(2.0 ** (-5.0 - jnp.arange(H, dtype=F32))))
    idx = jnp.arange(l, dtype=F32)
    rel = idx[:, None] - idx[None, :]
    intra = jnp.where(rel >= 0, jnp.exp(log_gamma[:, None, None] * jnp.maximum(rel, 0.0)), 0.0)
    q_dec = jnp.exp(log_gamma[None, :] * (idx[:, None] + 1.0))[None, :, :, None]
    k_dec = jnp.exp(log_gamma[None, :] * (l - 1.0 - idx[:, None]))[None, :, :, None]
    c_dec = jnp.exp(log_gamma * l)[None, :, None, None]

    def to_chunks(t):
        return t.reshape(b, c, l, H, d).transpose(1, 0, 2, 3, 4)

    def step(state, inp):
        qc, kc, vc = inp
        sc = jnp.einsum("bihd,bjhd->bhij", qc, kc) * intra
        o = jnp.einsum("bhij,bjhe->bihe", sc, vc) + jnp.einsum("bihd,bhde->bihe", qc, state) * q_dec
        state = state * c_dec + jnp.einsum("bjhd,bjhe->bhde", kc * k_dec, vc)
        return state, o

    _, o = lax.scan(step, jnp.zeros((b, H, d, d), F32), (to_chunks(q), to_chunks(k), to_chunks(v)))
    o = o.transpose(1, 0, 2, 3, 4).reshape(b, s, H, d)
    mu = jnp.mean(o, -1, keepdims=True)
    var = jnp.mean(jnp.square(o - mu), -1, keepdims=True)
    o = ((o - mu) * lax.rsqrt(var + NORM_EPS)).reshape(b, s, H * d) * gn_g + gn_b
    return (jax.nn.silu(g.astype(F32)) * o).astype(g.dtype)


def _linear_recurrence_combine(left, right):
    a1, b1 = left
    a2, b2 = right
    return a1 * a2, a2 * b1 + b2


def rglru_mixer(xb, gb, conv_w, conv_b, w_a, b_a, w_x, b_x, a_param):
    b, s, _ = xb.shape
    u = causal_dwconv(xb, conv_w, conv_b).astype(F32)
    ub = u.reshape(b, s, LRU_BLOCKS, LRU_BLOCK_DIM)
    r = jax.nn.sigmoid(jnp.einsum("bsgi,gij->bsgj", ub, w_a.astype(F32)).reshape(b, s, LRU_WIDTH) + b_a)
    i = jax.nn.sigmoid(jnp.einsum("bsgi,gij->bsgj", ub, w_x.astype(F32)).reshape(b, s, LRU_WIDTH) + b_x)
    log_a = -LRU_C * r * jax.nn.softplus(-a_param.astype(F32))
    a = jnp.exp(log_a)
    inp = jnp.sqrt(-jnp.expm1(2.0 * log_a)) * (i * u)
    _, h = lax.associative_scan(_linear_recurrence_combine, (a, inp), axis=1)
    return (h * jax.nn.gelu(gb.astype(F32))).astype(xb.dtype)


def swiglu(x, w_in, w_out):
    gu = x @ w_in
    return (jax.nn.silu(gu[..., :D_FF]) * gu[..., D_FF:]) @ w_out


def setup_inputs(seed: int = 0) -> dict:
    key = jax.random.key(seed)
    ks = iter(jax.random.split(key, 40))
    L = DEPTH

    def nrm(shape, scale):
        return jax.random.normal(next(ks), shape, F32) * scale

    def gain(shape):
        return 1.0 + nrm(shape, 0.02)

    x = jax.random.normal(next(ks), (BATCH, SEQ, D_MODEL), F32)
    positions = jnp.broadcast_to(jnp.arange(SEQ, dtype=jnp.int32)[None, :], (BATCH, SEQ))
    w_in = nrm((L, D_MODEL, N_IN), D_MODEL ** -0.5)
    mla_g_q = gain((L, MLA_Q_LORA))
    mla_w_uq = nrm((L, MLA_Q_LORA, MLA_HEADS * (MLA_NOPE_DIM + MLA_ROPE_DIM)), MLA_Q_LORA ** -0.5)
    mla_g_kv = gain((L, MLA_KV_LORA))
    mla_w_ukv = nrm((L, MLA_KV_LORA, MLA_HEADS * (MLA_NOPE_DIM + MLA_V_DIM)), MLA_KV_LORA ** -0.5)
    ssd_conv_w = nrm((L, SSD_CONV, SSD_XBC), SSD_CONV ** -0.5)
    ssd_conv_b = nrm((L, SSD_XBC), 0.02)
    u = jax.random.uniform(next(ks), (L, SSD_HEADS), F32)
    dt0 = jnp.exp(u * (math.log(0.1) - math.log(0.001)) + math.log(0.001))
    ssd_dt_bias = dt0 + jnp.log(-jnp.expm1(-dt0))
    ssd_a_log = jnp.log(jax.random.uniform(next(ks), (L, SSD_HEADS), F32, 1.0, 16.0))
    ssd_d = gain((L, SSD_HEADS))
    ssd_norm_g = gain((L, SSD_INNER))
    ret_gn_g = gain((L, GROUP_W))
    ret_gn_b = nrm((L, GROUP_W), 0.02)
    lru_conv_w = nrm((L, LRU_CONV, LRU_WIDTH), LRU_CONV ** -0.5)
    lru_conv_b = nrm((L, LRU_WIDTH), 0.02)
    lru_w_a = nrm((L, LRU_BLOCKS, LRU_BLOCK_DIM, LRU_BLOCK_DIM), LRU_BLOCK_DIM ** -0.5)
    lru_b_a = nrm((L, LRU_WIDTH), 0.02)
    lru_w_x = nrm((L, LRU_BLOCKS, LRU_BLOCK_DIM, LRU_BLOCK_DIM), LRU_BLOCK_DIM ** -0.5)
    lru_b_x = nrm((L, LRU_WIDTH), 0.02)
    a_c = jax.random.uniform(next(ks), (L, LRU_WIDTH), F32, 0.9, 0.999)
    a_s = a_c ** (1.0 / LRU_C)
    lru_a_param = jnp.log(a_s) - jnp.log1p(-a_s)
    w_out = nrm((L, D_MIX, D_MODEL), BETA * D_MIX ** -0.5)
    ln1_g = gain((L, D_MODEL))
    ln1_b = nrm((L, D_MODEL), 0.02)
    w_ffn_in = nrm((L, D_MODEL, 2 * D_FF), D_MODEL ** -0.5)
    w_ffn_out = nrm((L, D_FF, D_MODEL), BETA * D_FF ** -0.5)
    ln2_g = gain((L, D_MODEL))
    ln2_b = nrm((L, D_MODEL), 0.02)
    return {"x": x, "positions": positions, "w_in": w_in,
            "mla_g_q": mla_g_q, "mla_w_uq": mla_w_uq, "mla_g_kv": mla_g_kv, "mla_w_ukv": mla_w_ukv,
            "ssd_conv_w": ssd_conv_w, "ssd_conv_b": ssd_conv_b, "ssd_dt_bias": ssd_dt_bias,
            "ssd_a_log": ssd_a_log, "ssd_d": ssd_d, "ssd_norm_g": ssd_norm_g,
            "ret_gn_g": ret_gn_g, "ret_gn_b": ret_gn_b,
            "lru_conv_w": lru_conv_w, "lru_conv_b": lru_conv_b, "lru_w_a": lru_w_a, "lru_b_a": lru_b_a,
            "lru_w_x": lru_w_x, "lru_b_x": lru_b_x, "lru_a_param": lru_a_param,
            "w_out": w_out, "ln1_g": ln1_g, "ln1_b": ln1_b,
            "w_ffn_in": w_ffn_in, "w_ffn_out": w_ffn_out, "ln2_g": ln2_g, "ln2_b": ln2_b}


def reference(x, positions, w_in, mla_g_q, mla_w_uq, mla_g_kv, mla_w_ukv,
              ssd_conv_w, ssd_conv_b, ssd_dt_bias, ssd_a_log, ssd_d, ssd_norm_g,
              ret_gn_g, ret_gn_b,
              lru_conv_w, lru_conv_b, lru_w_a, lru_b_a, lru_w_x, lru_b_x, lru_a_param,
              w_out, ln1_g, ln1_b, w_ffn_in, w_ffn_out, ln2_g, ln2_b):
    splits = _in_splits()
    for l in range(DEPTH):
        h = x @ w_in[l]
        cq, ckv, kr, z, xbc, dt_raw, rq, rk, rv, rg, lx, lg = jnp.split(h, splits, axis=-1)
        y_a = mla_mixer(cq, ckv, kr, positions, mla_g_q[l], mla_w_uq[l], mla_g_kv[l], mla_w_ukv[l])
        y_b = ssd_mixer(z, xbc, dt_raw, ssd_conv_w[l], ssd_conv_b[l], ssd_dt_bias[l], ssd_a_log[l],
                        ssd_d[l], ssd_norm_g[l])
        y_c = retention_mixer(rq, rk, rv, rg, positions, ret_gn_g[l], ret_gn_b[l])
        y_d = rglru_mixer(lx, lg, lru_conv_w[l], lru_conv_b[l], lru_w_a[l], lru_b_a[l],
                          lru_w_x[l], lru_b_x[l], lru_a_param[l])
        mix = jnp.concatenate([y_a, y_b, y_c, y_d], axis=-1) @ w_out[l]
        x = layernorm(ALPHA * x + mix, ln1_g[l], ln1_b[l])
        x = layernorm(ALPHA * x + swiglu(x, w_ffn_in[l], w_ffn_out[l]), ln2_g[l], ln2_b[l])
    return x
```

```python
import functools
import math

import jax
import jax.numpy as jnp
import numpy as np
from jax import lax
from jax.experimental import pallas as pl
from jax.experimental.pallas import tpu as pltpu

F32 = jnp.float32
BF16 = jnp.bfloat16

GROUP_W = 256
HEADS = 4
HEAD_W = 64
MLA_NOPE = 32
MLA_ROPE = 16
MLA_QK = MLA_NOPE + MLA_ROPE
MLA_KV_LORA = 128
SSD_STATE = 128
SSD_XBC = 768
CHUNK = 128
CONV_K = 4
LRU_C = 8.0
ROPE_THETA = 10000.0
NORM_EPS = 1e-5
HALO = 8

LANES = 128
VMEM_LIMIT = 56 * 1024 * 1024

IN_MLA = 512
IN_ZX = 1024
IN_DT = 256
IN_RET = 1024
IN_LRU = 512
IN_PAD = IN_MLA + IN_ZX + IN_DT + IN_RET + IN_LRU


def _sigmoid(x):
    return 1.0 / (1.0 + jnp.exp(-x))


def _silu(x):
    return x * _sigmoid(x)


def _softplus(x):
    return jnp.maximum(x, 0.0) + jnp.log1p(jnp.exp(-jnp.abs(x)))


def _gelu_tanh(x):
    c = math.sqrt(2.0 / math.pi)
    return x * (0.5 * (1.0 + jnp.tanh(c * (x + 0.044715 * (x * x * x)))))


def _rms(x, g):
    return x * lax.rsqrt(jnp.mean(x * x, axis=-1, keepdims=True) + NORM_EPS) * g


def _layernorm(v, g, b):
    mu = jnp.mean(v, axis=-1, keepdims=True)
    d = v - mu
    var = jnp.mean(d * d, axis=-1, keepdims=True)
    return d * lax.rsqrt(var + NORM_EPS) * g + b


def _dot(a, b):
    return jnp.dot(a, b, preferred_element_type=F32)


def _dot_nt(a, b):
    return lax.dot_general(a, b, (((1,), (1,)), ((), ())), preferred_element_type=F32)


def _dot_f32(a, b):
    return jnp.dot(a, b, preferred_element_type=F32, precision=lax.Precision.HIGHEST)


def _resident(shape):
    nd = len(shape)
    return pl.BlockSpec(shape, lambda *_: (0,) * nd, pipeline_mode=pl.Buffered(1))


def _params(sem):
    return pltpu.CompilerParams(dimension_semantics=sem, vmem_limit_bytes=VMEM_LIMIT)


def _rope_tab_kernel(pos_ref, fm_ref, sm_ref, fr_ref, sr_ref, cm_ref, snm_ref, cr_ref, snr_ref):
    pos = pos_ref[...]
    am = pos * fm_ref[...]
    cm_ref[...] = jnp.cos(am)
    snm_ref[...] = jnp.sin(am) * sm_ref[...]
    ar = pos * fr_ref[...]
    cr_ref[...] = jnp.cos(ar)
    snr_ref[...] = jnp.sin(ar) * sr_ref[...]


def _rope_tables(pos, fm, sm, fr, sr, tm):
    t = pos.shape[0]
    row = pl.BlockSpec((1, LANES), lambda i: (0, 0))
    tab = pl.BlockSpec((tm, LANES), lambda i: (i, 0))
    return pl.pallas_call(
        _rope_tab_kernel,
        grid=(t // tm,),
        in_specs=[pl.BlockSpec((tm, 1), lambda i: (i, 0)), row, row, row, row],
        out_specs=[tab, tab, tab, tab],
        out_shape=[jax.ShapeDtypeStruct((t, LANES), F32)] * 4,
        compiler_params=_params(("parallel",)),
        name="rope_tables",
    )(pos, fm, sm, fr, sr)


def _in_proj_kernel(x_ref, w_ref, mla_ref, zx_ref, dt_ref, ret_ref, lru_ref):
    xb = x_ref[...].astype(BF16)
    start = 0
    for ref in (mla_ref, zx_ref, dt_ref, ret_ref, lru_ref):
        width = ref.shape[-1]
        ref[...] = _dot(xb, w_ref[:, start:start + width]).astype(ref.dtype)
        start += width


def _in_proj(x, w_pad, tm):
    t, d = x.shape
    widths = (IN_MLA, IN_ZX, IN_DT, IN_RET, IN_LRU)
    dtypes = (BF16, BF16, F32, BF16, BF16)
    return pl.pallas_call(
        _in_proj_kernel,
        grid=(t // tm,),
        in_specs=[pl.BlockSpec((tm, d), lambda i: (i, 0)), _resident(w_pad.shape)],
        out_specs=[pl.BlockSpec((tm, w), lambda i: (i, 0)) for w in widths],
        out_shape=[jax.ShapeDtypeStruct((t, w), dt) for w, dt in zip(widths, dtypes)],
        compiler_params=_params(("parallel",)),
        name="in_proj",
    )(x, w_pad)


def _swap_halves(x, lane_in_group, half):
    first = lane_in_group
    return jnp.where(first, pltpu.roll(x, LANES - half, 1), pltpu.roll(x, half, 1))


def _mla_prep_kernel(in_ref, cos_ref, sin_ref, gq_ref, gkv_ref, wq_ref, wk_ref, wv_ref, e_ref,
                     vone_ref, q_out, k_out, v_out):
    xin = in_ref[...]
    cq = xin[:, :GROUP_W].astype(F32)
    ckv = xin[:, GROUP_W:GROUP_W + MLA_KV_LORA].astype(F32)
    kr = xin[:, GROUP_W + MLA_KV_LORA:]
    nq = _rms(cq, gq_ref[...]).astype(BF16)
    nkv = _rms(ckv, gkv_ref[...]).astype(BF16)
    q = _dot(nq, wq_ref[...])
    k = _dot(nkv, wk_ref[...]) + _dot(kr, e_ref[...])
    v = _dot(nkv, wv_ref[...]) + vone_ref[...]
    cos = cos_ref[...]
    sin = sin_ref[...]
    lane = lax.broadcasted_iota(jnp.int32, cos.shape, 1)
    first = (lane % HEAD_W) < (MLA_NOPE + MLA_ROPE // 2)
    scale = MLA_QK ** -0.5
    for h in range(HEADS):
        sl = slice(LANES * h, LANES * (h + 1))
        qh = q[:, sl]
        kh = k[:, sl]
        qh = qh * cos + _swap_halves(qh, first, MLA_ROPE // 2) * sin
        kh = kh * cos + _swap_halves(kh, first, MLA_ROPE // 2) * sin
        q_out[0, h] = (qh * scale).astype(BF16)
        k_out[0, h] = kh.astype(BF16)
        v_out[0, h] = v[:, sl].astype(BF16)


def _mla_prep(mla_in, cos_m, sin_m, gq, gkv, wq, wk, wv, e_mat, vone, b, s, tm):
    ns = s // tm
    flat = lambda width: pl.BlockSpec((tm, width), lambda bi, i: (bi * ns + i, 0))
    out = pl.BlockSpec((1, HEADS, tm, LANES), lambda bi, i: (bi, 0, i, 0))
    return pl.pallas_call(
        _mla_prep_kernel,
        grid=(b, ns),
        in_specs=[flat(IN_MLA), flat(LANES), flat(LANES), _resident(gq.shape), _resident(gkv.shape),
                  _resident(wq.shape), _resident(wk.shape), _resident(wv.shape),
                  _resident(e_mat.shape), _resident(vone.shape)],
        out_specs=[out, out, out],
        out_shape=[jax.ShapeDtypeStruct((b, HEADS, s, LANES), BF16)] * 3,
        compiler_params=_params(("parallel", "parallel")),
        name="mla_prep",
    )(mla_in, cos_m, sin_m, gq, gkv, wq, wk, wv, e_mat, vone)


def _attn_kernel(qi_ref, kj_ref, q_ref, k_ref, v_ref, o_ref, m_ref, acc_ref):
    p = pl.program_id(1)
    qi = qi_ref[p]
    kj = kj_ref[p]

    @pl.when(kj == 0)
    def _():
        m_ref[...] = jnp.full(m_ref.shape, -jnp.inf, F32)
        acc_ref[...] = jnp.zeros(acc_ref.shape, F32)

    def step(masked):
        def head(h, carry):
            s = _dot_nt(q_ref[0, h], k_ref[0, h])
            if masked:
                row = lax.broadcasted_iota(jnp.int32, s.shape, 0)
                col = lax.broadcasted_iota(jnp.int32, s.shape, 1)
                s = jnp.where(row >= col, s, -jnp.inf)
            m_prev = m_ref[h]
            m_new = jnp.maximum(m_prev, jnp.max(s, axis=1, keepdims=True))
            alpha = jnp.exp(m_prev - m_new)
            pm = jnp.exp(s - m_new[:, :1]).astype(BF16)
            acc_ref[h] = alpha * acc_ref[h] + _dot(pm, v_ref[0, h])
            m_ref[h] = m_new
            return carry
        lax.fori_loop(0, HEADS, head, 0)

    @pl.when(kj < qi)
    def _():
        step(False)

    @pl.when(kj == qi)
    def _():
        step(True)
        outs = []
        for h in range(HEADS):
            a = acc_ref[h]
            outs.append(a[:, :HEAD_W] / a[:, HEAD_W:HEAD_W + 1])
        o_ref[0] = jnp.concatenate(outs, axis=1).astype(o_ref.dtype)


def _attention(q, k, v, tq):
    b, _, s, _ = q.shape
    nq = s // tq
    qi = np.concatenate([np.full(i + 1, i, np.int32) for i in range(nq)])
    kj = np.concatenate([np.arange(i + 1, dtype=np.int32) for i in range(nq)])
    blk = lambda tab: pl.BlockSpec((1, HEADS, tq, LANES),
                                   lambda bi, p, qi_r, kj_r: (bi, 0, (qi_r if tab == 0 else kj_r)[p], 0))
    grid_spec = pltpu.PrefetchScalarGridSpec(
        num_scalar_prefetch=2,
        grid=(b, len(qi)),
        in_specs=[blk(0), blk(1), blk(1)],
        out_specs=pl.BlockSpec((1, tq, GROUP_W), lambda bi, p, qi_r, kj_r: (bi, qi_r[p], 0)),
        scratch_shapes=[pltpu.VMEM((HEADS, tq, LANES), F32), pltpu.VMEM((HEADS, tq, LANES), F32)],
    )
    return pl.pallas_call(
        _attn_kernel,
        grid_spec=grid_spec,
        out_shape=jax.ShapeDtypeStruct((b, s, GROUP_W), BF16),
        compiler_params=_params(("parallel", "arbitrary")),
        name="mla_attention",
    )(jnp.asarray(qi), jnp.asarray(kj), q, k, v)


def _causal_conv(x, halo_ref, w, bias):
    ts = x.shape[0]
    xe = jnp.concatenate([halo_ref[...], x], axis=0)
    halo_ref[...] = x[ts - HALO:, :]
    y = w[CONV_K - 1:CONV_K, :] * x + bias
    for back in range(1, CONV_K):
        tap = w[CONV_K - 1 - back:CONV_K - back, :]
        y = y + tap * pltpu.roll(xe, back, 0)[HALO:, :]
    return y


def _ssd_kernel(zx_ref, dt_ref, cw_ref, cb_ref, dtb_ref, alog_ref, d_ref, g_ref, o_ref,
                halo_ref, st_ref):
    @pl.when(pl.program_id(1) == 0)
    def _():
        halo_ref[...] = jnp.zeros(halo_ref.shape, F32)
        st_ref[...] = jnp.zeros(st_ref.shape, F32)

    zx = zx_ref[...]
    ts = zx.shape[0]
    z = zx[:, :GROUP_W].astype(F32)
    xbc = _silu(_causal_conv(zx[:, GROUP_W:].astype(F32), halo_ref, cw_ref[...], cb_ref[...]))
    xs = xbc[:, :GROUP_W]
    bm = xbc[:, GROUP_W:2 * GROUP_W]
    cm = xbc[:, 2 * GROUP_W:]
    dt = _softplus(dt_ref[...] + dtb_ref[...])
    da = dt * (-jnp.exp(alog_ref[...]))
    dtx = dt * xs

    row = lax.broadcasted_iota(jnp.int32, (CHUNK, CHUNK), 0)
    col = lax.broadcasted_iota(jnp.int32, (CHUNK, CHUNK), 1)
    tril = row >= col
    trif = tril.astype(F32)
    lane_head = lax.broadcasted_iota(jnp.int32, (CHUNK, GROUP_W), 1) // HEAD_W

    for c in range(ts // CHUNK):
        r = slice(CHUNK * c, CHUNK * (c + 1))
        acs = _dot_f32(trif, da[r])
        acs_t = acs.T
        a_last = acs[CHUNK - 1:CHUNK, :]
        dtx_c = dtx[r]
        dtx_b = dtx_c.astype(BF16)
        w_end = (jnp.exp(a_last - acs) * dtx_c).astype(BF16)
        y_diag = jnp.zeros((CHUNK, GROUP_W), F32)
        y_off = []
        for g in range(2):
            gs = slice(SSD_STATE * g, SSD_STATE * (g + 1))
            bg = bm[r, gs]
            cg = cm[r, gs].astype(BF16)
            gram = _dot_nt(cg, bg.astype(BF16))
            for h in (2 * g, 2 * g + 1):
                seg = acs[:, HEAD_W * h:HEAD_W * h + 1] - acs_t[HEAD_W * h:HEAD_W * h + 1, :]
                decay = jnp.exp(jnp.where(tril, seg, -jnp.inf))
                yd = _dot((gram * decay).astype(BF16), dtx_b)
                y_diag = jnp.where(lane_head == h, yd, y_diag)
            prev = st_ref[g]
            y_off.append(_dot(cg, prev.astype(BF16)))
            st_ref[g] = prev * jnp.exp(a_last[:, gs]) + _dot(bg.T.astype(BF16), w_end[:, gs])
        y = y_diag + jnp.concatenate(y_off, axis=1) * jnp.exp(acs) + xs[r] * d_ref[...]
        y = y * _silu(z[r])
        o_ref[r, :] = _rms(y, g_ref[...]).astype(o_ref.dtype)


def _ssd(zx, dt, cw, cb, dtb, alog, dskip, g, b, s, ts):
    ns = s // ts
    flat = lambda width: pl.BlockSpec((ts, width), lambda bi, i: (bi * ns + i, 0))
    return pl.pallas_call(
        _ssd_kernel,
        grid=(b, ns),
        in_specs=[flat(IN_ZX), flat(IN_DT)] + [_resident(a.shape) for a in (cw, cb, dtb, alog, dskip, g)],
        out_specs=flat(GROUP_W),
        out_shape=jax.ShapeDtypeStruct((b * s, GROUP_W), BF16),
        scratch_shapes=[pltpu.VMEM((HALO, SSD_XBC), F32), pltpu.VMEM((2, SSD_STATE, 2 * HEAD_W), F32)],
        compiler_params=_params(("parallel", "arbitrary")),
        name="ssd",
    )(zx, dt, cw, cb, dtb, alog, dskip, g)


def _ret_kernel(in_ref, cos_ref, sin_ref, gng_ref, gnb_ref, o_ref, st_ref):
    @pl.when(pl.program_id(1) == 0)
    def _():
        st_ref[...] = jnp.zeros(st_ref.shape, F32)

    xin = in_ref[...]
    ts = xin.shape[0]
    cos = cos_ref[...]
    sin = sin_ref[...]
    lane = lax.broadcasted_iota(jnp.int32, cos.shape, 1)
    first = (lane % HEAD_W) < (HEAD_W // 2)

    def rope(x):
        parts = []
        for half in range(GROUP_W // LANES):
            xh = x[:, LANES * half:LANES * (half + 1)]
            parts.append(xh * cos + _swap_halves(xh, first, HEAD_W // 2) * sin)
        return jnp.concatenate(parts, axis=1)

    q = rope(xin[:, :GROUP_W].astype(F32))
    k = rope(xin[:, GROUP_W:2 * GROUP_W].astype(F32)) * (HEAD_W ** -0.5)
    v = xin[:, 2 * GROUP_W:3 * GROUP_W]
    gate = xin[:, 3 * GROUP_W:].astype(F32)

    log_gamma = [math.log1p(-(2.0 ** (-5.0 - h))) for h in range(HEADS)]
    lane_head = lax.broadcasted_iota(jnp.int32, (CHUNK, GROUP_W), 1) // HEAD_W
    lg_lane = jnp.full((CHUNK, GROUP_W), log_gamma[HEADS - 1], F32)
    for h in range(HEADS - 1):
        lg_lane = jnp.where(lane_head == h, log_gamma[h], lg_lane)
    idx = lax.broadcasted_iota(jnp.int32, (CHUNK, GROUP_W), 0).astype(F32)
    q_dec = jnp.exp(lg_lane * (idx + 1.0))
    k_dec = jnp.exp(lg_lane * (CHUNK - 1.0 - idx))
    c_dec = jnp.exp(lg_lane[:1, :] * float(CHUNK))
    rel = (lax.broadcasted_iota(jnp.int32, (CHUNK, CHUNK), 0)
           - lax.broadcasted_iota(jnp.int32, (CHUNK, CHUNK), 1)).astype(F32)
    blk_r = lax.broadcasted_iota(jnp.int32, (GROUP_W, GROUP_W), 0) // HEAD_W
    blk_c = lax.broadcasted_iota(jnp.int32, (GROUP_W, GROUP_W), 1) // HEAD_W
    same_head = blk_r == blk_c
    avg = jnp.where(same_head, 1.0 / HEAD_W, 0.0).astype(F32)

    for c in range(ts // CHUNK):
        r = slice(CHUNK * c, CHUNK * (c + 1))
        qc = q[r]
        kc = k[r]
        kb = kc.astype(BF16)
        vb = v[r]
        state = st_ref[...]
        o = _dot(qc.astype(BF16), state.astype(BF16)) * q_dec
        for h in range(HEADS):
            intra = jnp.where(rel >= 0.0, jnp.exp(log_gamma[h] * jnp.maximum(rel, 0.0)), 0.0)
            qh = jnp.where(lane_head == h, qc, 0.0).astype(BF16)
            sc = _dot_nt(qh, kb) * intra
            o = o + jnp.where(lane_head == h, _dot(sc.astype(BF16), vb), 0.0)
        upd = _dot((kc * k_dec).T.astype(BF16), vb)
        st_ref[...] = state * c_dec + jnp.where(same_head, upd, 0.0)
        mu = _dot_f32(o, avg)
        dlt = o - mu
        var = _dot_f32(dlt * dlt, avg)
        on = dlt * lax.rsqrt(var + NORM_EPS) * gng_ref[...] + gnb_ref[...]
        o_ref[r, :] = (_silu(gate[r]) * on).astype(o_ref.dtype)


def _retention(ret_in, cos_r, sin_r, gng, gnb, b, s, ts):
    ns = s // ts
    flat = lambda width: pl.BlockSpec((ts, width), lambda bi, i: (bi * ns + i, 0))
    return pl.pallas_call(
        _ret_kernel,
        grid=(b, ns),
        in_specs=[flat(IN_RET), flat(LANES), flat(LANES), _resident(gng.shape), _resident(gnb.shape)],
        out_specs=flat(GROUP_W),
        out_shape=jax.ShapeDtypeStruct((b * s, GROUP_W), BF16),
        scratch_shapes=[pltpu.VMEM((GROUP_W, GROUP_W), F32)],
        compiler_params=_params(("parallel", "arbitrary")),
        name="retention",
    )(ret_in, cos_r, sin_r, gng, gnb)


def _lru_kernel(in_ref, cw_ref, cb_ref, wa_ref, ba_ref, wx_ref, bx_ref, ap_ref, o_ref,
                halo_ref, h_ref):
    @pl.when(pl.program_id(1) == 0)
    def _():
        halo_ref[...] = jnp.zeros(halo_ref.shape, F32)
        h_ref[...] = jnp.zeros(h_ref.shape, F32)

    xin = in_ref[...]
    ts = xin.shape[0]
    u = _causal_conv(xin[:, :GROUP_W].astype(F32), halo_ref, cw_ref[...], cb_ref[...])
    ub = u.astype(BF16)
    r = _sigmoid(_dot(ub, wa_ref[...]) + ba_ref[...])
    gate_in = _sigmoid(_dot(ub, wx_ref[...]) + bx_ref[...])
    log_a = -LRU_C * r * _softplus(-ap_ref[...])
    a = jnp.exp(log_a)
    b = jnp.sqrt(-jnp.tanh(log_a) * (a * a + 1.0)) * (gate_in * u)

    row = lax.broadcasted_iota(jnp.int32, a.shape, 0)
    stride = 1
    while stride < ts:
        live = row >= stride
        a_sh = jnp.where(live, pltpu.roll(a, stride, 0), 1.0)
        b_sh = jnp.where(live, pltpu.roll(b, stride, 0), 0.0)
        b = a * b_sh + b
        a = a * a_sh
        stride *= 2
    h = a * h_ref[0:1, :] + b
    h_ref[...] = jnp.broadcast_to(h[ts - 1:ts, :], h_ref.shape)
    o_ref[...] = (h * _gelu_tanh(xin[:, GROUP_W:].astype(F32))).astype(o_ref.dtype)


def _rglru(lru_in, cw, cb, wa, ba, wx, bx, ap, b, s, ts):
    ns = s // ts
    flat = lambda width: pl.BlockSpec((ts, width), lambda bi, i: (bi * ns + i, 0))
    return pl.pallas_call(
        _lru_kernel,
        grid=(b, ns),
        in_specs=[flat(IN_LRU)] + [_resident(a.shape) for a in (cw, cb, wa, ba, wx, bx, ap)],
        out_specs=flat(GROUP_W),
        out_shape=jax.ShapeDtypeStruct((b * s, GROUP_W), BF16),
        scratch_shapes=[pltpu.VMEM((HALO, GROUP_W), F32), pltpu.VMEM((HALO, GROUP_W), F32)],
        compiler_params=_params(("parallel", "arbitrary")),
        name="rglru",
    )(lru_in, cw, cb, wa, ba, wx, bx, ap)


def _out_ln_kernel(ya_ref, yb_ref, yc_ref, yd_ref, x_ref, w_ref, g_ref, b_ref, o_ref, *, alpha):
    mix = _dot(ya_ref[...], w_ref[0:GROUP_W, :])
    for n, ref in enumerate((yb_ref, yc_ref, yd_ref), start=1):
        mix = mix + _dot(ref[...], w_ref[GROUP_W * n:GROUP_W * (n + 1), :])
    o_ref[...] = _layernorm(alpha * x_ref[...] + mix, g_ref[...], b_ref[...])


def _out_ln(ys, x, w, g, bias, alpha, tm):
    t, d = x.shape
    ytile = pl.BlockSpec((tm, GROUP_W), lambda i: (i, 0))
    xtile = pl.BlockSpec((tm, d), lambda i: (i, 0))
    return pl.pallas_call(
        functools.partial(_out_ln_kernel, alpha=alpha),
        grid=(t // tm,),
        in_specs=[ytile] * 4 + [xtile, _resident(w.shape), _resident(g.shape), _resident(bias.shape)],
        out_specs=xtile,
        out_shape=jax.ShapeDtypeStruct((t, d), F32),
        compiler_params=_params(("parallel",)),
        name="out_proj_ln",
    )(*ys, x, w, g, bias)


FFN_CHUNK = 256


def _ffn_ln_kernel(x_ref, wi_ref, wo_ref, g_ref, b_ref, o_ref, *, alpha):
    x = x_ref[...]
    xb = x.astype(BF16)
    d_ff = wo_ref.shape[0]
    acc = jnp.zeros(x.shape, F32)
    for c in range(d_ff // FFN_CHUNK):
        lo = FFN_CHUNK * c
        gte = _dot(xb, wi_ref[:, lo:lo + FFN_CHUNK])
        up = _dot(xb, wi_ref[:, d_ff + lo:d_ff + lo + FFN_CHUNK])
        acc = acc + _dot((_silu(gte) * up).astype(BF16), wo_ref[lo:lo + FFN_CHUNK, :])
    o_ref[...] = _layernorm(alpha * x + acc, g_ref[...], b_ref[...])


def _ffn_ln(x, wi, wo, g, bias, alpha, tm):
    t, d = x.shape
    assert wo.shape[0] % FFN_CHUNK == 0
    xtile = pl.BlockSpec((tm, d), lambda i: (i, 0))
    return pl.pallas_call(
        functools.partial(_ffn_ln_kernel, alpha=alpha),
        grid=(t // tm,),
        in_specs=[xtile, _resident(wi.shape), _resident(wo.shape), _resident(g.shape), _resident(bias.shape)],
        out_specs=xtile,
        out_shape=jax.ShapeDtypeStruct((t, d), F32),
        compiler_params=_params(("parallel",)),
        name="ffn_ln",
    )(x, wi, wo, g, bias)


def _pad_cols(w, width):
    return jnp.pad(w, ((0, 0), (0, width - w.shape[1])))


def _pack_in_proj(w):
    cuts = np.cumsum([256, 128, 16, 256, 768, 4, 256, 256, 256, 256, 256, 256])[:-1]
    cq, ckv, kr, z, xbc, dt, rq, rk, rv, rg, lx, lg = jnp.split(w, [int(c) for c in cuts], axis=1)
    segs = [cq, ckv, _pad_cols(kr, LANES), z, xbc, jnp.repeat(dt, HEAD_W, axis=1), rq, rk, rv, rg, lx, lg]
    out = jnp.concatenate(segs, axis=1).astype(BF16)
    assert out.shape[1] == IN_PAD
    return out


def _per_head_lanes(w, width):
    k = w.shape[0]
    w = w.reshape(k, HEADS, width)
    return jnp.pad(w, ((0, 0), (0, 0), (0, LANES - width))).reshape(k, HEADS * LANES)


def _row(v):
    return v.reshape(1, -1).astype(F32)


def _head_row(v):
    return jnp.repeat(v.astype(F32), HEAD_W).reshape(1, GROUP_W)


def _block_diag(w):
    nb, n, _ = w.shape
    eye = jnp.eye(nb, dtype=w.dtype)
    return jnp.einsum("gij,gh->gihj", w, eye).reshape(nb * n, nb * n)


def _rope_rows():
    inv_m = ROPE_THETA ** (-jnp.arange(0, MLA_ROPE, 2, dtype=F32) / MLA_ROPE)
    inv_r = ROPE_THETA ** (-jnp.arange(0, HEAD_W, 2, dtype=F32) / HEAD_W)
    zeros = lambda n: jnp.zeros((n,), F32)
    ones = lambda n: jnp.ones((n,), F32)
    fm = jnp.concatenate([zeros(MLA_NOPE), inv_m, inv_m, zeros(HEAD_W - MLA_QK)])
    sm = jnp.concatenate([zeros(MLA_NOPE), -ones(8), ones(8), zeros(HEAD_W - MLA_QK)])
    fm = jnp.concatenate([fm, zeros(LANES - HEAD_W)])
    sm = jnp.concatenate([sm, zeros(LANES - HEAD_W)])
    fr = jnp.tile(jnp.concatenate([inv_r, inv_r]), LANES // HEAD_W)
    sr = jnp.tile(jnp.concatenate([-ones(HEAD_W // 2), ones(HEAD_W // 2)]), LANES // HEAD_W)
    return [a.reshape(1, LANES) for a in (fm, sm, fr, sr)]


def _forward(x, positions, p, tiles):
    b, s, d = x.shape
    t = b * s
    depth = p["w_in"].shape[0]
    alpha = (2.0 * depth) ** 0.25
    xf = x.reshape(t, d)
    pos = positions.reshape(t, 1).astype(F32)
    cos_m, sin_m, cos_r, sin_r = _rope_tables(pos, *_rope_rows(), tiles["rope"])

    e_mat = np.zeros((LANES, HEADS * LANES), np.float32)
    vone = np.zeros((1, HEADS * LANES), np.float32)
    for h in range(HEADS):
        for j in range(MLA_ROPE):
            e_mat[j, LANES * h + MLA_NOPE + j] = 1.0
        vone[0, LANES * h + HEAD_W] = 1.0
    e_mat = jnp.asarray(e_mat, BF16)
    vone = jnp.asarray(vone, F32)

    for l in range(depth):
        mla_in, zx, dt, ret_in, lru_in = _in_proj(xf, _pack_in_proj(p["w_in"][l]), tiles["proj"])

        w_ukv = p["mla_w_ukv"][l].reshape(MLA_KV_LORA, HEADS, MLA_NOPE + HEAD_W)
        wq = _per_head_lanes(p["mla_w_uq"][l], MLA_QK).astype(BF16)
        wk = _per_head_lanes(w_ukv[:, :, :MLA_NOPE].reshape(MLA_KV_LORA, -1), MLA_NOPE).astype(BF16)
        wv = _per_head_lanes(w_ukv[:, :, MLA_NOPE:].reshape(MLA_KV_LORA, -1), HEAD_W).astype(BF16)
        q, k, v = _mla_prep(mla_in, cos_m, sin_m, _row(p["mla_g_q"][l]), _row(p["mla_g_kv"][l]),
                            wq, wk, wv, e_mat, vone, b, s, tiles["prep"])
        y_a = _attention(q, k, v, tiles["attn"]).reshape(t, GROUP_W)

        y_b = _ssd(zx, dt, p["ssd_conv_w"][l].astype(F32), _row(p["ssd_conv_b"][l]),
                   _head_row(p["ssd_dt_bias"][l]), _head_row(p["ssd_a_log"][l]), _head_row(p["ssd_d"][l]),
                   _row(p["ssd_norm_g"][l]), b, s, tiles["seq"])
        y_c = _retention(ret_in, cos_r, sin_r, _row(p["ret_gn_g"][l]), _row(p["ret_gn_b"][l]),
                         b, s, tiles["seq"])
        y_d = _rglru(lru_in, p["lru_conv_w"][l].astype(F32), _row(p["lru_conv_b"][l]),
                     _block_diag(p["lru_w_a"][l]).astype(BF16), _row(p["lru_b_a"][l]),
                     _block_diag(p["lru_w_x"][l]).astype(BF16), _row(p["lru_b_x"][l]),
                     _row(p["lru_a_param"][l]), b, s, tiles["seq"])

        x1 = _out_ln((y_a, y_b, y_c, y_d), xf, p["w_out"][l].astype(BF16), _row(p["ln1_g"][l]),
                     _row(p["ln1_b"][l]), alpha, tiles["proj"])
        xf = _ffn_ln(x1, p["w_ffn_in"][l].astype(BF16), p["w_ffn_out"][l].astype(BF16),
                     _row(p["ln2_g"][l]), _row(p["ln2_b"][l]), alpha, tiles["proj"])
    return xf.reshape(b, s, d)


def _tiles(s):
    return {"rope": min(1024, s), "proj": min(512, s), "prep": min(512, s),
            "attn": min(1024, s), "seq": min(256, s)}


def kernel(x, positions, w_in, mla_g_q, mla_w_uq, mla_g_kv, mla_w_ukv, ssd_conv_w, ssd_conv_b, ssd_dt_bias, ssd_a_log, ssd_d, ssd_norm_g, ret_gn_g, ret_gn_b, lru_conv_w, lru_conv_b, lru_w_a, lru_b_a, lru_w_x, lru_b_x, lru_a_param, w_out, ln1_g, ln1_b, w_ffn_in, w_ffn_out, ln2_g, ln2_b):
    p = dict(w_in=w_in, mla_g_q=mla_g_q, mla_w_uq=mla_w_uq, mla_g_kv=mla_g_kv, mla_w_ukv=mla_w_ukv,
             ssd_conv_w=ssd_conv_w, ssd_conv_b=ssd_conv_b, ssd_dt_bias=ssd_dt_bias, ssd_a_log=ssd_a_log,
             ssd_d=ssd_d, ssd_norm_g=ssd_norm_g, ret_gn_g=ret_gn_g, ret_gn_b=ret_gn_b,
             lru_conv_w=lru_conv_w, lru_conv_b=lru_conv_b, lru_w_a=lru_w_a, lru_b_a=lru_b_a,
             lru_w_x=lru_w_x, lru_b_x=lru_b_x, lru_a_param=lru_a_param, w_out=w_out, ln1_g=ln1_g,
             ln1_b=ln1_b, w_ffn_in=w_ffn_in, w_ffn_out=w_ffn_out, ln2_g=ln2_g, ln2_b=ln2_b)
    return _forward(x, positions, p, _tiles(x.shape[1]))
```

```python
import functools
import math

import jax
import jax.numpy as jnp
import numpy as np
from jax import lax
from jax.experimental import pallas as pl
from jax.experimental.pallas import tpu as pltpu

F32 = jnp.float32
BF16 = jnp.bfloat16

GROUP_W = 256
HEADS = 4
HEAD_W = 64
MLA_NOPE = 32
MLA_ROPE = 16
MLA_QK = MLA_NOPE + MLA_ROPE
MLA_KV_LORA = 128
SSD_STATE = 128
SSD_XBC = 768
CHUNK = 128
CONV_K = 4
LRU_C = 8.0
ROPE_THETA = 10000.0
NORM_EPS = 1e-5
HALO = 8

LANES = 128
VMEM_LIMIT = 56 * 1024 * 1024

IN_MLA = 512
IN_ZX = 1024
IN_DT = 256
IN_RET = 1024
IN_LRU = 512
IN_PAD = IN_MLA + IN_ZX + IN_DT + IN_RET + IN_LRU


def _sigmoid(x):
    return 1.0 / (1.0 + jnp.exp(-x))


def _silu(x):
    return x * _sigmoid(x)


def _softplus(x):
    return jnp.maximum(x, 0.0) + jnp.log1p(jnp.exp(-jnp.abs(x)))


def _gelu_tanh(x):
    c = math.sqrt(2.0 / math.pi)
    return x * (0.5 * (1.0 + jnp.tanh(c * (x + 0.044715 * (x * x * x)))))


def _rms(x, g):
    return x * lax.rsqrt(jnp.mean(x * x, axis=-1, keepdims=True) + NORM_EPS) * g


def _layernorm(v, g, b):
    mu = jnp.mean(v, axis=-1, keepdims=True)
    d = v - mu
    var = jnp.mean(d * d, axis=-1, keepdims=True)
    return d * lax.rsqrt(var + NORM_EPS) * g + b


def _dot(a, b):
    return jnp.dot(a, b, preferred_element_type=F32)


def _dot_nt(a, b):
    return lax.dot_general(a, b, (((1,), (1,)), ((), ())), preferred_element_type=F32)


def _dot_f32(a, b):
    return jnp.dot(a, b, preferred_element_type=F32, precision=lax.Precision.HIGHEST)


def _resident(shape):
    nd = len(shape)
    return pl.BlockSpec(shape, lambda *_: (0,) * nd, pipeline_mode=pl.Buffered(1))


def _params(sem):
    return pltpu.CompilerParams(dimension_semantics=sem, vmem_limit_bytes=VMEM_LIMIT)


def _rope_tab_kernel(pos_ref, fm_ref, sm_ref, fr_ref, sr_ref, cm_ref, snm_ref, cr_ref, snr_ref):
    pos = pos_ref[...]
    am = pos * fm_ref[...]
    cm_ref[...] = jnp.cos(am)
    snm_ref[...] = jnp.sin(am) * sm_ref[...]
    ar = pos * fr_ref[...]
    cr_ref[...] = jnp.cos(ar)
    snr_ref[...] = jnp.sin(ar) * sr_ref[...]


def _rope_tables(pos, fm, sm, fr, sr, tm):
    t = pos.shape[0]
    row = pl.BlockSpec((1, LANES), lambda i: (0, 0))
    tab = pl.BlockSpec((tm, LANES), lambda i: (i, 0))
    return pl.pallas_call(
        _rope_tab_kernel,
        grid=(t // tm,),
        in_specs=[pl.BlockSpec((tm, 1), lambda i: (i, 0)), row, row, row, row],
        out_specs=[tab, tab, tab, tab],
        out_shape=[jax.ShapeDtypeStruct((t, LANES), F32)] * 4,
        compiler_params=_params(("parallel",)),
        name="rope_tables",
    )(pos, fm, sm, fr, sr)


def _in_proj_kernel(x_ref, w_ref, mla_ref, zx_ref, dt_ref, ret_ref, lru_ref):
    xb = x_ref[...].astype(BF16)
    start = 0
    for ref in (mla_ref, zx_ref, dt_ref, ret_ref, lru_ref):
        width = ref.shape[-1]
        ref[...] = _dot(xb, w_ref[:, start:start + width]).astype(ref.dtype)
        start += width


def _in_proj(x, w_pad, tm):
    t, d = x.shape
    widths = (IN_MLA, IN_ZX, IN_DT, IN_RET, IN_LRU)
    dtypes = (BF16, BF16, F32, BF16, BF16)
    return pl.pallas_call(
        _in_proj_kernel,
        grid=(t // tm,),
        in_specs=[pl.BlockSpec((tm, d), lambda i: (i, 0)), _resident(w_pad.shape)],
        out_specs=[pl.BlockSpec((tm, w), lambda i: (i, 0)) for w in widths],
        out_shape=[jax.ShapeDtypeStruct((t, w), dt) for w, dt in zip(widths, dtypes)],
        compiler_params=_params(("parallel",)),
        name="in_proj",
    )(x, w_pad)


def _swap_halves(x, lane_in_group, half):
    first = lane_in_group
    return jnp.where(first, pltpu.roll(x, LANES - half, 1), pltpu.roll(x, half, 1))


def _mla_prep_kernel(in_ref, cos_ref, sin_ref, gq_ref, gkv_ref, wq_ref, wk_ref, wv_ref, e_ref,
                     vone_ref, q_out, k_out, vt_out):
    xin = in_ref[...]
    cq = xin[:, :GROUP_W].astype(F32)
    ckv = xin[:, GROUP_W:GROUP_W + MLA_KV_LORA].astype(F32)
    kr = xin[:, GROUP_W + MLA_KV_LORA:]
    nq = _rms(cq, gq_ref[...]).astype(BF16)
    nkv = _rms(ckv, gkv_ref[...]).astype(BF16)
    q = _dot(nq, wq_ref[...])
    k = _dot(nkv, wk_ref[...]) + _dot(kr, e_ref[...])
    v = _dot(nkv, wv_ref[...]) + vone_ref[...]
    cos = cos_ref[...]
    sin = sin_ref[...]
    lane = lax.broadcasted_iota(jnp.int32, cos.shape, 1)
    first = (lane % HEAD_W) < (MLA_NOPE + MLA_ROPE // 2)
    scale = MLA_QK ** -0.5 * math.log2(math.e)
    for h in range(HEADS):
        sl = slice(LANES * h, LANES * (h + 1))
        qh = q[:, sl]
        kh = k[:, sl]
        qh = qh * cos + _swap_halves(qh, first, MLA_ROPE // 2) * sin
        kh = kh * cos + _swap_halves(kh, first, MLA_ROPE // 2) * sin
        q_out[0, h] = (qh * scale).astype(BF16)
        k_out[0, h] = kh.astype(BF16)
        vt_out[0, h, 0] = v[:, sl].T.astype(BF16)


def _mla_prep(mla_in, cos_m, sin_m, gq, gkv, wq, wk, wv, e_mat, vone, b, s, tm):
    ns = s // tm
    flat = lambda width: pl.BlockSpec((tm, width), lambda bi, i: (bi * ns + i, 0))
    out = pl.BlockSpec((1, HEADS, tm, LANES), lambda bi, i: (bi, 0, i, 0))
    out_t = pl.BlockSpec((1, HEADS, 1, LANES, tm), lambda bi, i: (bi, 0, i, 0, 0))
    qk_shape = jax.ShapeDtypeStruct((b, HEADS, s, LANES), BF16)
    return pl.pallas_call(
        _mla_prep_kernel,
        grid=(b, ns),
        in_specs=[flat(IN_MLA), flat(LANES), flat(LANES), _resident(gq.shape), _resident(gkv.shape),
                  _resident(wq.shape), _resident(wk.shape), _resident(wv.shape),
                  _resident(e_mat.shape), _resident(vone.shape)],
        out_specs=[out, out, out_t],
        out_shape=[qk_shape, qk_shape, jax.ShapeDtypeStruct((b, HEADS, ns, LANES, tm), BF16)],
        compiler_params=_params(("parallel", "parallel")),
        name="mla_prep",
    )(mla_in, cos_m, sin_m, gq, gkv, wq, wk, wv, e_mat, vone)


def _attn_kernel(q_ref, k_ref, vt_ref, o_ref, sa_ref, sb_ref, mca_ref, mcb_ref, m_ref, acc_ref):
    qi = pl.program_id(2)
    t = q_ref.shape[2]
    q = q_ref[0, 0]

    m_ref[...] = jnp.full(m_ref.shape, -jnp.inf, F32)
    acc_ref[...] = jnp.zeros(acc_ref.shape, F32)

    def fill(s_ref, mc_ref, tile, masked):
        kt = k_ref[0, 0, pl.ds(pl.multiple_of(tile * t, t), t), :]
        st = _dot_nt(kt, q)
        if masked:
            key = lax.broadcasted_iota(jnp.int32, st.shape, 0)
            qry = lax.broadcasted_iota(jnp.int32, st.shape, 1)
            st = jnp.where(qry >= key, st, -jnp.inf)
        s_ref[...] = st
        mc_ref[...] = jnp.broadcast_to(jnp.max(st, axis=0, keepdims=True), mc_ref.shape)

    def drain(s_ref, mc_ref, tile):
        m_prev = m_ref[0:1, :]
        m_new = jnp.maximum(m_prev, mc_ref[0:1, :])
        alpha = jnp.exp2(m_prev - m_new)
        pt = jnp.exp2(s_ref[...] - m_new).astype(BF16)
        acc_ref[...] = alpha * acc_ref[...] + _dot(vt_ref[0, 0, tile], pt)
        m_ref[...] = jnp.broadcast_to(m_new, m_ref.shape)

    fill(sa_ref, mca_ref, qi, True)

    def pair(i, carry):
        fill(sb_ref, mcb_ref, 2 * i, False)
        drain(sa_ref, mca_ref, jnp.where(i == 0, qi, 2 * i - 1))
        fill(sa_ref, mca_ref, 2 * i + 1, False)
        drain(sb_ref, mcb_ref, 2 * i)
        return carry

    n_pairs = qi // 2
    lax.fori_loop(0, n_pairs, pair, 0)
    last_a = jnp.where(n_pairs == 0, qi, 2 * n_pairs - 1)

    @pl.when(qi % 2 == 0)
    def _():
        drain(sa_ref, mca_ref, last_a)

    @pl.when(qi % 2 == 1)
    def _():
        fill(sb_ref, mcb_ref, qi - 1, False)
        drain(sa_ref, mca_ref, last_a)
        drain(sb_ref, mcb_ref, qi - 1)

    acc = acc_ref[...]
    out_t = acc[:HEAD_W, :] / acc[HEAD_W:HEAD_W + 1, :]
    o_ref[0, 0] = out_t.T.astype(o_ref.dtype)


def _attention(q, k, vt, t):
    b, heads, s, _ = q.shape
    nt = s // t
    assert vt.shape == (b, heads, nt, LANES, t)
    return pl.pallas_call(
        _attn_kernel,
        grid=(b, heads, nt),
        in_specs=[pl.BlockSpec((1, 1, t, LANES), lambda bi, h, i: (bi, h, i, 0)),
                  pl.BlockSpec((1, 1, s, LANES), lambda bi, h, i: (bi, h, 0, 0)),
                  pl.BlockSpec((1, 1, nt, LANES, t), lambda bi, h, i: (bi, h, 0, 0, 0))],
        out_specs=pl.BlockSpec((1, 1, t, HEAD_W), lambda bi, h, i: (bi, h, i, 0)),
        out_shape=jax.ShapeDtypeStruct((b, heads, s, HEAD_W), BF16),
        scratch_shapes=[pltpu.VMEM((t, t), F32), pltpu.VMEM((t, t), F32),
                        pltpu.VMEM((HALO, t), F32), pltpu.VMEM((HALO, t), F32),
                        pltpu.VMEM((HALO, t), F32), pltpu.VMEM((LANES, t), F32)],
        compiler_params=_params(("parallel", "parallel", "arbitrary")),
        name="mla_attention",
    )(q, k, vt)


def _causal_conv(x, halo_ref, w, bias):
    ts = x.shape[0]
    xe = jnp.concatenate([halo_ref[...], x], axis=0)
    halo_ref[...] = x[ts - HALO:, :]
    y = w[CONV_K - 1:CONV_K, :] * x + bias
    for back in range(1, CONV_K):
        tap = w[CONV_K - 1 - back:CONV_K - back, :]
        y = y + tap * pltpu.roll(xe, back, 0)[HALO:, :]
    return y


def _ssd_kernel(zx_ref, dt_ref, cw_ref, cb_ref, dtb_ref, alog_ref, d_ref, g_ref, o_ref,
                halo_ref, st_ref):
    @pl.when(pl.program_id(1) == 0)
    def _():
        halo_ref[...] = jnp.zeros(halo_ref.shape, F32)
        st_ref[...] = jnp.zeros(st_ref.shape, F32)

    zx = zx_ref[...]
    ts = zx.shape[0]
    z = zx[:, :GROUP_W].astype(F32)
    xbc = _silu(_causal_conv(zx[:, GROUP_W:].astype(F32), halo_ref, cw_ref[...], cb_ref[...]))
    xs = xbc[:, :GROUP_W]
    bm = xbc[:, GROUP_W:2 * GROUP_W]
    cm = xbc[:, 2 * GROUP_W:]
    dt = _softplus(dt_ref[...] + dtb_ref[...])
    da = dt * (-jnp.exp(alog_ref[...]))
    dtx = dt * xs

    row = lax.broadcasted_iota(jnp.int32, (CHUNK, CHUNK), 0)
    col = lax.broadcasted_iota(jnp.int32, (CHUNK, CHUNK), 1)
    tril = row >= col
    trif = tril.astype(F32)
    lane_head = lax.broadcasted_iota(jnp.int32, (CHUNK, GROUP_W), 1) // HEAD_W

    for c in range(ts // CHUNK):
        r = slice(CHUNK * c, CHUNK * (c + 1))
        acs = _dot_f32(trif, da[r])
        acs_t = acs.T
        a_last = acs[CHUNK - 1:CHUNK, :]
        dtx_c = dtx[r]
        dtx_b = dtx_c.astype(BF16)
        w_end = (jnp.exp(a_last - acs) * dtx_c).astype(BF16)
        y_diag = jnp.zeros((CHUNK, GROUP_W), F32)
        y_off = []
        for g in range(2):
            gs = slice(SSD_STATE * g, SSD_STATE * (g + 1))
            bg = bm[r, gs]
            cg = cm[r, gs].astype(BF16)
            gram = _dot_nt(cg, bg.astype(BF16))
            for h in (2 * g, 2 * g + 1):
                seg = acs[:, HEAD_W * h:HEAD_W * h + 1] - acs_t[HEAD_W * h:HEAD_W * h + 1, :]
                decay = jnp.exp(jnp.where(tril, seg, -jnp.inf))
                yd = _dot((gram * decay).astype(BF16), dtx_b)
                y_diag = jnp.where(lane_head == h, yd, y_diag)
            prev = st_ref[g]
            y_off.append(_dot(cg, prev.astype(BF16)))
            st_ref[g] = prev * jnp.exp(a_last[:, gs]) + _dot(bg.T.astype(BF16), w_end[:, gs])
        y = y_diag + jnp.concatenate(y_off, axis=1) * jnp.exp(acs) + xs[r] * d_ref[...]
        y = y * _silu(z[r])
        o_ref[r, :] = _rms(y, g_ref[...]).astype(o_ref.dtype)


def _ssd(zx, dt, cw, cb, dtb, alog, dskip, g, b, s, ts):
    ns = s // ts
    flat = lambda width: pl.BlockSpec((ts, width), lambda bi, i: (bi * ns + i, 0))
    return pl.pallas_call(
        _ssd_kernel,
        grid=(b, ns),
        in_specs=[flat(IN_ZX), flat(IN_DT)] + [_resident(a.shape) for a in (cw, cb, dtb, alog, dskip, g)],
        out_specs=flat(GROUP_W),
        out_shape=jax.ShapeDtypeStruct((b * s, GROUP_W), BF16),
        scratch_shapes=[pltpu.VMEM((HALO, SSD_XBC), F32), pltpu.VMEM((2, SSD_STATE, 2 * HEAD_W), F32)],
        compiler_params=_params(("parallel", "arbitrary")),
        name="ssd",
    )(zx, dt, cw, cb, dtb, alog, dskip, g)


def _ret_kernel(in_ref, cos_ref, sin_ref, gng_ref, gnb_ref, o_ref, st_ref):
    @pl.when(pl.program_id(1) == 0)
    def _():
        st_ref[...] = jnp.zeros(st_ref.shape, F32)

    xin = in_ref[...]
    ts = xin.shape[0]
    cos = cos_ref[...]
    sin = sin_ref[...]
    lane = lax.broadcasted_iota(jnp.int32, cos.shape, 1)
    first = (lane % HEAD_W) < (HEAD_W // 2)

    def rope(x):
        parts = []
        for half in range(GROUP_W // LANES):
            xh = x[:, LANES * half:LANES * (half + 1)]
            parts.append(xh * cos + _swap_halves(xh, first, HEAD_W // 2) * sin)
        return jnp.concatenate(parts, axis=1)

    q = rope(xin[:, :GROUP_W].astype(F32))
    k = rope(xin[:, GROUP_W:2 * GROUP_W].astype(F32)) * (HEAD_W ** -0.5)
    v = xin[:, 2 * GROUP_W:3 * GROUP_W]
    gate = xin[:, 3 * GROUP_W:].astype(F32)

    log_gamma = [math.log1p(-(2.0 ** (-5.0 - h))) for h in range(HEADS)]
    lane_head = lax.broadcasted_iota(jnp.int32, (CHUNK, GROUP_W), 1) // HEAD_W
    lg_lane = jnp.full((CHUNK, GROUP_W), log_gamma[HEADS - 1], F32)
    for h in range(HEADS - 1):
        lg_lane = jnp.where(lane_head == h, log_gamma[h], lg_lane)
    idx = lax.broadcasted_iota(jnp.int32, (CHUNK, GROUP_W), 0).astype(F32)
    q_dec = jnp.exp(lg_lane * (idx + 1.0))
    k_dec = jnp.exp(lg_lane * (CHUNK - 1.0 - idx))
    c_dec = jnp.exp(lg_lane[:1, :] * float(CHUNK))
    rel = (lax.broadcasted_iota(jnp.int32, (CHUNK, CHUNK), 0)
           - lax.broadcasted_iota(jnp.int32, (CHUNK, CHUNK), 1)).astype(F32)
    blk_r = lax.broadcasted_iota(jnp.int32, (GROUP_W, GROUP_W), 0) // HEAD_W
    blk_c = lax.broadcasted_iota(jnp.int32, (GROUP_W, GROUP_W), 1) // HEAD_W
    same_head = blk_r == blk_c
    avg = jnp.where(same_head, 1.0 / HEAD_W, 0.0).astype(F32)

    for c in range(ts // CHUNK):
        r = slice(CHUNK * c, CHUNK * (c + 1))
        qc = q[r]
        kc = k[r]
        kb = kc.astype(BF16)
        vb = v[r]
        state = st_ref[...]
        o = _dot(qc.astype(BF16), state.astype(BF16)) * q_dec
        for h in range(HEADS):
            intra = jnp.where(rel >= 0.0, jnp.exp(log_gamma[h] * jnp.maximum(rel, 0.0)), 0.0)
            qh = jnp.where(lane_head == h, qc, 0.0).astype(BF16)
            sc = _dot_nt(qh, kb) * intra
            o = o + jnp.where(lane_head == h, _dot(sc.astype(BF16), vb), 0.0)
        upd = _dot((kc * k_dec).T.astype(BF16), vb)
        st_ref[...] = state * c_dec + jnp.where(same_head, upd, 0.0)
        mu = _dot_f32(o, avg)
        dlt = o - mu
        var = _dot_f32(dlt * dlt, avg)
        on = dlt * lax.rsqrt(var + NORM_EPS) * gng_ref[...] + gnb_ref[...]
        o_ref[r, :] = (_silu(gate[r]) * on).astype(o_ref.dtype)


def _retention(ret_in, cos_r, sin_r, gng, gnb, b, s, ts):
    ns = s // ts
    flat = lambda width: pl.BlockSpec((ts, width), lambda bi, i: (bi * ns + i, 0))
    return pl.pallas_call(
        _ret_kernel,
        grid=(b, ns),
        in_specs=[flat(IN_RET), flat(LANES), flat(LANES), _resident(gng.shape), _resident(gnb.shape)],
        out_specs=flat(GROUP_W),
        out_shape=jax.ShapeDtypeStruct((b * s, GROUP_W), BF16),
        scratch_shapes=[pltpu.VMEM((GROUP_W, GROUP_W), F32)],
        compiler_params=_params(("parallel", "arbitrary")),
        name="retention",
    )(ret_in, cos_r, sin_r, gng, gnb)


def _lru_kernel(in_ref, cw_ref, cb_ref, wa_ref, ba_ref, wx_ref, bx_ref, ap_ref, o_ref,
                halo_ref, h_ref):
    @pl.when(pl.program_id(1) == 0)
    def _():
        halo_ref[...] = jnp.zeros(halo_ref.shape, F32)
        h_ref[...] = jnp.zeros(h_ref.shape, F32)

    xin = in_ref[...]
    ts = xin.shape[0]
    u = _causal_conv(xin[:, :GROUP_W].astype(F32), halo_ref, cw_ref[...], cb_ref[...])
    ub = u.astype(BF16)
    r = _sigmoid(_dot(ub, wa_ref[...]) + ba_ref[...])
    gate_in = _sigmoid(_dot(ub, wx_ref[...]) + bx_ref[...])
    log_a = -LRU_C * r * _softplus(-ap_ref[...])
    a = jnp.exp(log_a)
    b = jnp.sqrt(-jnp.tanh(log_a) * (a * a + 1.0)) * (gate_in * u)

    row = lax.broadcasted_iota(jnp.int32, a.shape, 0)
    stride = 1
    while stride < ts:
        live = row >= stride
        a_sh = jnp.where(live, pltpu.roll(a, stride, 0), 1.0)
        b_sh = jnp.where(live, pltpu.roll(b, stride, 0), 0.0)
        b = a * b_sh + b
        a = a * a_sh
        stride *= 2
    h = a * h_ref[0:1, :] + b
    h_ref[...] = jnp.broadcast_to(h[ts - 1:ts, :], h_ref.shape)
    o_ref[...] = (h * _gelu_tanh(xin[:, GROUP_W:].astype(F32))).astype(o_ref.dtype)


def _rglru(lru_in, cw, cb, wa, ba, wx, bx, ap, b, s, ts):
    ns = s // ts
    flat = lambda width: pl.BlockSpec((ts, width), lambda bi, i: (bi * ns + i, 0))
    return pl.pallas_call(
        _lru_kernel,
        grid=(b, ns),
        in_specs=[flat(IN_LRU)] + [_resident(a.shape) for a in (cw, cb, wa, ba, wx, bx, ap)],
        out_specs=flat(GROUP_W),
        out_shape=jax.ShapeDtypeStruct((b * s, GROUP_W), BF16),
        scratch_shapes=[pltpu.VMEM((HALO, GROUP_W), F32), pltpu.VMEM((HALO, GROUP_W), F32)],
        compiler_params=_params(("parallel", "arbitrary")),
        name="rglru",
    )(lru_in, cw, cb, wa, ba, wx, bx, ap)


def _out_ln_kernel(ya_ref, yb_ref, yc_ref, yd_ref, x_ref, w_ref, g_ref, b_ref, o_ref, *, alpha):
    ya = jnp.concatenate([ya_ref[0, h] for h in range(HEADS)], axis=1)
    mix = _dot(ya, w_ref[0:GROUP_W, :])
    for n, ref in enumerate((yb_ref, yc_ref, yd_ref), start=1):
        mix = mix + _dot(ref[...], w_ref[GROUP_W * n:GROUP_W * (n + 1), :])
    o_ref[...] = _layernorm(alpha * x_ref[...] + mix, g_ref[...], b_ref[...])


def _out_ln(ys, x, w, g, bias, alpha, tm):
    t, d = x.shape
    ns = ys[0].shape[2] // tm
    atile = pl.BlockSpec((1, HEADS, tm, HEAD_W), lambda i: (i // ns, 0, i % ns, 0))
    ytile = pl.BlockSpec((tm, GROUP_W), lambda i: (i, 0))
    xtile = pl.BlockSpec((tm, d), lambda i: (i, 0))
    return pl.pallas_call(
        functools.partial(_out_ln_kernel, alpha=alpha),
        grid=(t // tm,),
        in_specs=[atile] + [ytile] * 3 + [xtile, _resident(w.shape), _resident(g.shape), _resident(bias.shape)],
        out_specs=xtile,
        out_shape=jax.ShapeDtypeStruct((t, d), F32),
        compiler_params=_params(("parallel",)),
        name="out_proj_ln",
    )(*ys, x, w, g, bias)


FFN_CHUNK = 256


def _ffn_ln_kernel(x_ref, wi_ref, wo_ref, g_ref, b_ref, o_ref, *, alpha):
    x = x_ref[...]
    xb = x.astype(BF16)
    d_ff = wo_ref.shape[0]
    acc = jnp.zeros(x.shape, F32)
    for c in range(d_ff // FFN_CHUNK):
        lo = FFN_CHUNK * c
        gte = _dot(xb, wi_ref[:, lo:lo + FFN_CHUNK])
        up = _dot(xb, wi_ref[:, d_ff + lo:d_ff + lo + FFN_CHUNK])
        acc = acc + _dot((_silu(gte) * up).astype(BF16), wo_ref[lo:lo + FFN_CHUNK, :])
    o_ref[...] = _layernorm(alpha * x + acc, g_ref[...], b_ref[...])


def _ffn_ln(x, wi, wo, g, bias, alpha, tm):
    t, d = x.shape
    assert wo.shape[0] % FFN_CHUNK == 0
    xtile = pl.BlockSpec((tm, d), lambda i: (i, 0))
    return pl.pallas_call(
        functools.partial(_ffn_ln_kernel, alpha=alpha),
        grid=(t // tm,),
        in_specs=[xtile, _resident(wi.shape), _resident(wo.shape), _resident(g.shape), _resident(bias.shape)],
        out_specs=xtile,
        out_shape=jax.ShapeDtypeStruct((t, d), F32),
        compiler_params=_params(("parallel",)),
        name="ffn_ln",
    )(x, wi, wo, g, bias)


SRC_KR = GROUP_W + MLA_KV_LORA
SRC_ZX = SRC_KR + MLA_ROPE
SRC_DT = SRC_ZX + IN_ZX
SRC_REST = SRC_DT + HEADS
N_IN = SRC_REST + IN_RET + IN_LRU


def _pack_in_proj_kernel(w_ref, o_ref):
    rows = o_ref.shape[0]
    o_ref[:, 0:SRC_KR] = w_ref[0, :, 0:SRC_KR].astype(BF16)
    kr = w_ref[0, :, SRC_KR:SRC_ZX].astype(BF16)
    o_ref[:, SRC_KR:IN_MLA] = jnp.concatenate(
        [kr, jnp.zeros((rows, IN_MLA - SRC_ZX), BF16)], axis=1)
    o_ref[:, IN_MLA:IN_MLA + IN_ZX] = w_ref[0, :, SRC_ZX:SRC_DT].astype(BF16)
    dt0 = IN_MLA + IN_ZX
    for h in range(HEADS):
        col = w_ref[0, :, SRC_DT + h:SRC_DT + h + 1]
        o_ref[:, dt0 + HEAD_W * h:dt0 + HEAD_W * (h + 1)] = jnp.broadcast_to(col, (rows, HEAD_W)).astype(BF16)
    o_ref[:, dt0 + IN_DT:IN_PAD] = w_ref[0, :, SRC_REST:N_IN].astype(BF16)


def _pack_in_proj(w_in, layer):
    _, d, n = w_in.shape
    assert n == N_IN
    return pl.pallas_call(
        _pack_in_proj_kernel,
        grid=(1,),
        in_specs=[pl.BlockSpec((1, d, n), lambda i: (layer, 0, 0))],
        out_specs=pl.BlockSpec((d, IN_PAD), lambda i: (0, 0)),
        out_shape=jax.ShapeDtypeStruct((d, IN_PAD), BF16),
        compiler_params=_params(("arbitrary",)),
        name="pack_in_proj",
    )(w_in)


def _per_head_lanes(w, width):
    k = w.shape[0]
    w = w.reshape(k, HEADS, width)
    return jnp.pad(w, ((0, 0), (0, 0), (0, LANES - width))).reshape(k, HEADS * LANES)


def _row(v):
    return v.reshape(1, -1).astype(F32)


def _head_row(v):
    return jnp.repeat(v.astype(F32), HEAD_W).reshape(1, GROUP_W)


def _block_diag(w):
    nb, n, _ = w.shape
    eye = jnp.eye(nb, dtype=w.dtype)
    return jnp.einsum("gij,gh->gihj", w, eye).reshape(nb * n, nb * n)


def _rope_rows():
    inv_m = ROPE_THETA ** (-jnp.arange(0, MLA_ROPE, 2, dtype=F32) / MLA_ROPE)
    inv_r = ROPE_THETA ** (-jnp.arange(0, HEAD_W, 2, dtype=F32) / HEAD_W)
    zeros = lambda n: jnp.zeros((n,), F32)
    ones = lambda n: jnp.ones((n,), F32)
    fm = jnp.concatenate([zeros(MLA_NOPE), inv_m, inv_m, zeros(HEAD_W - MLA_QK)])
    sm = jnp.concatenate([zeros(MLA_NOPE), -ones(8), ones(8), zeros(HEAD_W - MLA_QK)])
    fm = jnp.concatenate([fm, zeros(LANES - HEAD_W)])
    sm = jnp.concatenate([sm, zeros(LANES - HEAD_W)])
    fr = jnp.tile(jnp.concatenate([inv_r, inv_r]), LANES // HEAD_W)
    sr = jnp.tile(jnp.concatenate([-ones(HEAD_W // 2), ones(HEAD_W // 2)]), LANES // HEAD_W)
    return [a.reshape(1, LANES) for a in (fm, sm, fr, sr)]


def _forward(x, positions, p, tiles):
    b, s, d = x.shape
    t = b * s
    depth = p["w_in"].shape[0]
    alpha = (2.0 * depth) ** 0.25
    xf = x.reshape(t, d)
    pos = positions.reshape(t, 1).astype(F32)
    cos_m, sin_m, cos_r, sin_r = _rope_tables(pos, *_rope_rows(), tiles["rope"])

    e_mat = np.zeros((LANES, HEADS * LANES), np.float32)
    vone = np.zeros((1, HEADS * LANES), np.float32)
    for h in range(HEADS):
        for j in range(MLA_ROPE):
            e_mat[j, LANES * h + MLA_NOPE + j] = 1.0
        vone[0, LANES * h + HEAD_W] = 1.0
    e_mat = jnp.asarray(e_mat, BF16)
    vone = jnp.asarray(vone, F32)

    for l in range(depth):
        mla_in, zx, dt, ret_in, lru_in = _in_proj(xf, _pack_in_proj(p["w_in"], l), tiles["proj"])

        w_ukv = p["mla_w_ukv"][l].reshape(MLA_KV_LORA, HEADS, MLA_NOPE + HEAD_W)
        wq = _per_head_lanes(p["mla_w_uq"][l], MLA_QK).astype(BF16)
        wk = _per_head_lanes(w_ukv[:, :, :MLA_NOPE].reshape(MLA_KV_LORA, -1), MLA_NOPE).astype(BF16)
        wv = _per_head_lanes(w_ukv[:, :, MLA_NOPE:].reshape(MLA_KV_LORA, -1), HEAD_W).astype(BF16)
        q, k, v = _mla_prep(mla_in, cos_m, sin_m, _row(p["mla_g_q"][l]), _row(p["mla_g_kv"][l]),
                            wq, wk, wv, e_mat, vone, b, s, tiles["attn"])
        y_a = _attention(q, k, v, tiles["attn"])

        y_b = _ssd(zx, dt, p["ssd_conv_w"][l].astype(F32), _row(p["ssd_conv_b"][l]),
                   _head_row(p["ssd_dt_bias"][l]), _head_row(p["ssd_a_log"][l]), _head_row(p["ssd_d"][l]),
                   _row(p["ssd_norm_g"][l]), b, s, tiles["seq"])
        y_c = _retention(ret_in, cos_r, sin_r, _row(p["ret_gn_g"][l]), _row(p["ret_gn_b"][l]),
                         b, s, tiles["seq"])
        y_d = _rglru(lru_in, p["lru_conv_w"][l].astype(F32), _row(p["lru_conv_b"][l]),
                     _block_diag(p["lru_w_a"][l]).astype(BF16), _row(p["lru_b_a"][l]),
                     _block_diag(p["lru_w_x"][l]).astype(BF16), _row(p["lru_b_x"][l]),
                     _row(p["lru_a_param"][l]), b, s, tiles["seq"])

        x1 = _out_ln((y_a, y_b, y_c, y_d), xf, p["w_out"][l].astype(BF16), _row(p["ln1_g"][l]),
                     _row(p["ln1_b"][l]), alpha, tiles["proj"])
        xf = _ffn_ln(x1, p["w_ffn_in"][l].astype(BF16), p["w_ffn_out"][l].astype(BF16),
                     _row(p["ln2_g"][l]), _row(p["ln2_b"][l]), alpha, tiles["proj"])
    return xf.reshape(b, s, d)


def _tiles(s):
    return {"rope": min(1024, s), "proj": min(512, s), "attn": min(1024, s), "seq": min(256, s)}


def kernel(x, positions, w_in, mla_g_q, mla_w_uq, mla_g_kv, mla_w_ukv, ssd_conv_w, ssd_conv_b, ssd_dt_bias, ssd_a_log, ssd_d, ssd_norm_g, ret_gn_g, ret_gn_b, lru_conv_w, lru_conv_b, lru_w_a, lru_b_a, lru_w_x, lru_b_x, lru_a_param, w_out, ln1_g, ln1_b, w_ffn_in, w_ffn_out, ln2_g, ln2_b):
    p = dict(w_in=w_in, mla_g_q=mla_g_q, mla_w_uq=mla_w_uq, mla_g_kv=mla_g_kv, mla_w_ukv=mla_w_ukv,
             ssd_conv_w=ssd_conv_w, ssd_conv_b=ssd_conv_b, ssd_dt_bias=ssd_dt_bias, ssd_a_log=ssd_a_log,
             ssd_d=ssd_d, ssd_norm_g=ssd_norm_g, ret_gn_g=ret_gn_g, ret_gn_b=ret_gn_b,
             lru_conv_w=lru_conv_w, lru_conv_b=lru_conv_b, lru_w_a=lru_w_a, lru_b_a=lru_b_a,
             lru_w_x=lru_w_x, lru_b_x=lru_b_x, lru_a_param=lru_a_param, w_out=w_out, ln1_g=ln1_g,
             ln1_b=ln1_b, w_ffn_in=w_ffn_in, w_ffn_out=w_ffn_out, ln2_g=ln2_g, ln2_b=ln2_b)
    return _forward(x, positions, p, _tiles(x.shape[1]))
```

```python
import functools
import math

import jax
import jax.numpy as jnp
import numpy as np
from jax import lax
from jax.experimental import pallas as pl
from jax.experimental.pallas import tpu as pltpu

F32 = jnp.float32
BF16 = jnp.bfloat16

GROUP_W = 256
HEADS = 4
HEAD_W = 64
MLA_NOPE = 32
MLA_ROPE = 16
MLA_QK = MLA_NOPE + MLA_ROPE
MLA_KV_LORA = 128
SSD_STATE = 128
SSD_XBC = 768
CHUNK = 128
CONV_K = 4
LRU_C = 8.0
ROPE_THETA = 10000.0
NORM_EPS = 1e-5
HALO = 8
VT_ROWS = 80
ATTN_CHUNK = 256

LANES = 128
VMEM_LIMIT = 56 * 1024 * 1024

IN_MLA = 512
IN_ZX = 1024
IN_DT = 256
IN_RET = 1024
IN_LRU = 512
IN_PAD = IN_MLA + IN_ZX + IN_DT + IN_RET + IN_LRU


def _sigmoid(x):
    return 1.0 / (1.0 + jnp.exp(-x))


def _silu(x):
    return x * _sigmoid(x)


def _softplus(x):
    return jnp.maximum(x, 0.0) + jnp.log1p(jnp.exp(-jnp.abs(x)))


def _gelu_tanh(x):
    c = math.sqrt(2.0 / math.pi)
    return x * (0.5 * (1.0 + jnp.tanh(c * (x + 0.044715 * (x * x * x)))))


def _rms(x, g):
    return x * lax.rsqrt(jnp.mean(x * x, axis=-1, keepdims=True) + NORM_EPS) * g


def _layernorm(v, g, b):
    mu = jnp.mean(v, axis=-1, keepdims=True)
    d = v - mu
    var = jnp.mean(d * d, axis=-1, keepdims=True)
    return d * lax.rsqrt(var + NORM_EPS) * g + b


def _dot(a, b):
    return jnp.dot(a, b, preferred_element_type=F32)


def _dot_nt(a, b):
    return lax.dot_general(a, b, (((1,), (1,)), ((), ())), preferred_element_type=F32)


def _split3(x):
    hi = x.astype(BF16)
    rest = x - hi.astype(F32)
    mid = rest.astype(BF16)
    lo = (rest - mid.astype(F32)).astype(BF16)
    return hi, mid, lo


def _dot_f32_rhs(c, x):
    return sum(_dot(c, part) for part in _split3(x))


def _dot_f32_lhs(x, c):
    return sum(_dot(part, c) for part in _split3(x))


def _resident(shape):
    nd = len(shape)
    return pl.BlockSpec(shape, lambda *_: (0,) * nd, pipeline_mode=pl.Buffered(1))


def _params(sem):
    return pltpu.CompilerParams(dimension_semantics=sem, vmem_limit_bytes=VMEM_LIMIT)


def _rope_tab_kernel(pos_ref, fm_ref, sm_ref, fr_ref, sr_ref, cm_ref, snm_ref, cr_ref, snr_ref):
    pos = pos_ref[...]
    am = pos * fm_ref[...]
    cm_ref[...] = jnp.cos(am)
    snm_ref[...] = jnp.sin(am) * sm_ref[...]
    ar = pos * fr_ref[...]
    cr_ref[...] = jnp.cos(ar)
    snr_ref[...] = jnp.sin(ar) * sr_ref[...]


def _rope_tables(pos, fm, sm, fr, sr, tm):
    t = pos.shape[0]
    row = pl.BlockSpec((1, LANES), lambda i: (0, 0))
    tab = pl.BlockSpec((tm, LANES), lambda i: (i, 0))
    return pl.pallas_call(
        _rope_tab_kernel,
        grid=(t // tm,),
        in_specs=[pl.BlockSpec((tm, 1), lambda i: (i, 0)), row, row, row, row],
        out_specs=[tab, tab, tab, tab],
        out_shape=[jax.ShapeDtypeStruct((t, LANES), F32)] * 4,
        compiler_params=_params(("parallel",)),
        name="rope_tables",
    )(pos, fm, sm, fr, sr)


def _in_proj_kernel(x_ref, w_ref, mla_ref, zx_ref, dt_ref, ret_ref, lru_ref):
    xb = x_ref[...].astype(BF16)
    start = 0
    for ref in (mla_ref, zx_ref, dt_ref, ret_ref, lru_ref):
        width = ref.shape[-1]
        ref[...] = _dot(xb, w_ref[:, start:start + width]).astype(ref.dtype)
        start += width


def _in_proj(x, w_pad, tm):
    t, d = x.shape
    widths = (IN_MLA, IN_ZX, IN_DT, IN_RET, IN_LRU)
    dtypes = (BF16, BF16, F32, BF16, BF16)
    return pl.pallas_call(
        _in_proj_kernel,
        grid=(t // tm,),
        in_specs=[pl.BlockSpec((tm, d), lambda i: (i, 0)), _resident(w_pad.shape)],
        out_specs=[pl.BlockSpec((tm, w), lambda i: (i, 0)) for w in widths],
        out_shape=[jax.ShapeDtypeStruct((t, w), dt) for w, dt in zip(widths, dtypes)],
        compiler_params=_params(("parallel",)),
        name="in_proj",
    )(x, w_pad)


def _swap_halves(x, lane_in_group, half):
    first = lane_in_group
    return jnp.where(first, pltpu.roll(x, LANES - half, 1), pltpu.roll(x, half, 1))


def _mla_prep_kernel(in_ref, cos_ref, sin_ref, gq_ref, gkv_ref, wq_ref, wk_ref, wv_ref, e_ref,
                     vone_ref, q_out, k_out, vt_out):
    xin = in_ref[...]
    cq = xin[:, :GROUP_W].astype(F32)
    ckv = xin[:, GROUP_W:GROUP_W + MLA_KV_LORA].astype(F32)
    kr = xin[:, GROUP_W + MLA_KV_LORA:]
    nq = _rms(cq, gq_ref[...]).astype(BF16)
    nkv = _rms(ckv, gkv_ref[...]).astype(BF16)
    q = _dot(nq, wq_ref[...])
    k = _dot(nkv, wk_ref[...]) + _dot(kr, e_ref[...])
    v = _dot(nkv, wv_ref[...]) + vone_ref[...]
    cos = cos_ref[...]
    sin = sin_ref[...]
    lane = lax.broadcasted_iota(jnp.int32, cos.shape, 1)
    first = (lane % HEAD_W) < (MLA_NOPE + MLA_ROPE // 2)
    scale = MLA_QK ** -0.5 * math.log2(math.e)
    for h in range(HEADS):
        sl = slice(LANES * h, LANES * (h + 1))
        qh = q[:, sl]
        kh = k[:, sl]
        qh = qh * cos + _swap_halves(qh, first, MLA_ROPE // 2) * sin
        kh = kh * cos + _swap_halves(kh, first, MLA_ROPE // 2) * sin
        q_out[0, h] = (qh * scale).astype(BF16)
        k_out[0, h] = kh.astype(BF16)
        vt_out[0, h, 0] = v[:, sl].T[:VT_ROWS, :].astype(BF16)


def _mla_prep(mla_in, cos_m, sin_m, gq, gkv, wq, wk, wv, e_mat, vone, b, s, tm):
    ns = s // tm
    flat = lambda width: pl.BlockSpec((tm, width), lambda bi, i: (bi * ns + i, 0))
    out = pl.BlockSpec((1, HEADS, tm, LANES), lambda bi, i: (bi, 0, i, 0))
    out_t = pl.BlockSpec((1, HEADS, 1, VT_ROWS, tm), lambda bi, i: (bi, 0, i, 0, 0))
    qk_shape = jax.ShapeDtypeStruct((b, HEADS, s, LANES), BF16)
    return pl.pallas_call(
        _mla_prep_kernel,
        grid=(b, ns),
        in_specs=[flat(IN_MLA), flat(LANES), flat(LANES), _resident(gq.shape), _resident(gkv.shape),
                  _resident(wq.shape), _resident(wk.shape), _resident(wv.shape),
                  _resident(e_mat.shape), _resident(vone.shape)],
        out_specs=[out, out, out_t],
        out_shape=[qk_shape, qk_shape, jax.ShapeDtypeStruct((b, HEADS, ns, VT_ROWS, tm), BF16)],
        compiler_params=_params(("parallel", "parallel")),
        name="mla_prep",
    )(mla_in, cos_m, sin_m, gq, gkv, wq, wk, wv, e_mat, vone)


def _attn_kernel(q_ref, k_ref, vt_ref, o_ref, sa_ref, sb_ref, sc_ref, mca_ref, mcb_ref, mcc_ref,
                 m_ref, acc_ref, *, t):
    nt = q_ref.shape[2] // t
    ch = min(ATTN_CHUNK, t)

    def reset():
        m_ref[...] = jnp.full(m_ref.shape, -jnp.inf, F32)
        acc_ref[...] = jnp.zeros(acc_ref.shape, F32)

    def q_tile(qt):
        return q_ref[0, 0, pl.ds(pl.multiple_of(qt * t, t), t), :]

    def scores(q, kt_tile, c, masked):
        kt = k_ref[0, 0, pl.ds(pl.multiple_of(kt_tile * t + ch * c, ch), ch), :]
        st = _dot_nt(kt, q)
        if masked:
            key = lax.broadcasted_iota(jnp.int32, st.shape, 0) + ch * c
            qry = lax.broadcasted_iota(jnp.int32, st.shape, 1)
            st = jnp.where(qry >= key, st, -jnp.inf)
        return st

    def fill(s_ref, mc_ref, qt, kt_tile, masked):
        q = q_tile(qt)
        col_max = None
        for c in range(t // ch):
            st = scores(q, kt_tile, c, masked)
            s_ref[ch * c:ch * (c + 1), :] = st
            cm = jnp.max(st, axis=0, keepdims=True)
            col_max = cm if col_max is None else jnp.maximum(col_max, cm)
        mc_ref[...] = jnp.broadcast_to(col_max, mc_ref.shape)

    def drain(s_ref, mc_ref, kt_tile):
        m_prev = m_ref[0:1, :]
        m_new = jnp.maximum(m_prev, mc_ref[0:1, :])
        alpha = jnp.exp2(m_prev - m_new)
        pt = jnp.exp2(s_ref[...] - m_new).astype(BF16)
        acc_ref[...] = alpha * acc_ref[...] + _dot(vt_ref[0, 0, kt_tile], pt)
        m_ref[...] = jnp.broadcast_to(m_new, m_ref.shape)

    def fill_drain(fs_ref, fmc_ref, f_qt, f_kt, masked, ds_ref, dmc_ref, d_kt):
        q = q_tile(f_qt)
        m_prev = m_ref[0:1, :]
        m_new = jnp.maximum(m_prev, dmc_ref[0:1, :])
        alpha = jnp.exp2(m_prev - m_new)
        col_max = None
        pv = None
        for c in range(t // ch):
            rows = slice(ch * c, ch * (c + 1))
            st = scores(q, f_kt, c, masked)
            fs_ref[rows, :] = st
            cm = jnp.max(st, axis=0, keepdims=True)
            col_max = cm if col_max is None else jnp.maximum(col_max, cm)
            pt = jnp.exp2(ds_ref[rows, :] - m_new).astype(BF16)
            part = _dot(vt_ref[0, 0, d_kt, :, rows], pt)
            pv = part if pv is None else pv + part
        fmc_ref[...] = jnp.broadcast_to(col_max, fmc_ref.shape)
        acc_ref[...] = alpha * acc_ref[...] + pv
        m_ref[...] = jnp.broadcast_to(m_new, m_ref.shape)

    def finish(qi):
        acc = acc_ref[...]
        out_t = acc[:HEAD_W, :] / acc[HEAD_W:HEAD_W + 1, :]
        o_ref[0, 0, pl.ds(pl.multiple_of(qi * t, t), t), :] = out_t.T.astype(o_ref.dtype)
        reset()

    reset()
    fill(sc_ref, mcc_ref, 0, 0, True)
    drain(sc_ref, mcc_ref, 0)
    finish(0)
    if nt > 1:
        fill(sc_ref, mcc_ref, 1, 1, True)

    def query_tile(qi, carry):
        nxt = jnp.minimum(qi + 1, nt - 1)
        fill_drain(sa_ref, mca_ref, qi, 0, False, sc_ref, mcc_ref, qi)

        def pair(i, c2):
            fill_drain(sb_ref, mcb_ref, qi, 2 * i + 1, False, sa_ref, mca_ref, 2 * i)
            fill_drain(sa_ref, mca_ref, qi, 2 * i + 2, False, sb_ref, mcb_ref, 2 * i + 1)
            return c2

        lax.fori_loop(0, (qi - 1) // 2, pair, 0)

        @pl.when(qi % 2 == 1)
        def _():
            fill_drain(sc_ref, mcc_ref, nxt, nxt, True, sa_ref, mca_ref, qi - 1)
            finish(qi)

        @pl.when(qi % 2 == 0)
        def _():
            fill_drain(sb_ref, mcb_ref, qi, qi - 1, False, sa_ref, mca_ref, qi - 2)
            fill_drain(sc_ref, mcc_ref, nxt, nxt, True, sb_ref, mcb_ref, qi - 1)
            finish(qi)

        return carry

    lax.fori_loop(1, nt, query_tile, 0)


def _attention(q, k, vt, t):
    b, heads, s, _ = q.shape
    nt = s // t
    assert vt.shape == (b, heads, nt, VT_ROWS, t)
    seq = lambda width: pl.BlockSpec((1, 1, s, width), lambda bi, h: (bi, h, 0, 0),
                                     pipeline_mode=pl.Buffered(1))
    return pl.pallas_call(
        functools.partial(_attn_kernel, t=t),
        grid=(b, heads),
        in_specs=[seq(LANES), seq(LANES),
                  pl.BlockSpec((1, 1, nt, VT_ROWS, t), lambda bi, h: (bi, h, 0, 0, 0),
                               pipeline_mode=pl.Buffered(1))],
        out_specs=pl.BlockSpec((1, 1, s, HEAD_W), lambda bi, h: (bi, h, 0, 0)),
        out_shape=jax.ShapeDtypeStruct((b, heads, s, HEAD_W), BF16),
        scratch_shapes=[pltpu.VMEM((t, t), F32)] * 3 + [pltpu.VMEM((HALO, t), F32)] * 4
                       + [pltpu.VMEM((VT_ROWS, t), F32)],
        compiler_params=_params(("parallel", "parallel")),
        name="mla_attention",
    )(q, k, vt)


def _causal_conv(x, xe_ref, w, bias):
    ts = x.shape[0]
    xe_ref[HALO:, :] = x
    y = w[CONV_K - 1:CONV_K, :] * x + bias
    for back in range(1, CONV_K):
        tap = w[CONV_K - 1 - back:CONV_K - back, :]
        y = y + tap * xe_ref[pl.ds(HALO - back, ts), :]
    xe_ref[0:HALO, :] = x[ts - HALO:, :]
    return y


def _ssd_kernel(zx_ref, dt_ref, cw_ref, cb_ref, dtb_ref, alog_ref, d_ref, g_ref, o_ref,
                halo_ref, st_ref):
    @pl.when(pl.program_id(1) == 0)
    def _():
        halo_ref[0:HALO, :] = jnp.zeros((HALO, halo_ref.shape[1]), F32)
        st_ref[...] = jnp.zeros(st_ref.shape, F32)

    zx = zx_ref[...]
    ts = zx.shape[0]
    z = zx[:, :GROUP_W].astype(F32)
    xbc = _silu(_causal_conv(zx[:, GROUP_W:].astype(F32), halo_ref, cw_ref[...], cb_ref[...]))
    xs = xbc[:, :GROUP_W]
    bm = xbc[:, GROUP_W:2 * GROUP_W]
    cm = xbc[:, 2 * GROUP_W:]
    dt = _softplus(dt_ref[...] + dtb_ref[...])
    da = dt * (-jnp.exp(alog_ref[...]))
    dtx = dt * xs

    row = lax.broadcasted_iota(jnp.int32, (CHUNK, CHUNK), 0)
    col = lax.broadcasted_iota(jnp.int32, (CHUNK, CHUNK), 1)
    tril = row >= col
    tril_b = tril.astype(BF16)
    lane_head = lax.broadcasted_iota(jnp.int32, (CHUNK, GROUP_W), 1) // HEAD_W

    for c in range(ts // CHUNK):
        r = slice(CHUNK * c, CHUNK * (c + 1))
        acs = _dot_f32_rhs(tril_b, da[r])
        acs_t = acs.T
        a_last = acs[CHUNK - 1:CHUNK, :]
        dtx_c = dtx[r]
        dtx_b = dtx_c.astype(BF16)
        w_end = (jnp.exp(a_last - acs) * dtx_c).astype(BF16)
        y_diag = jnp.zeros((CHUNK, GROUP_W), F32)
        y_off = []
        for g in range(2):
            gs = slice(SSD_STATE * g, SSD_STATE * (g + 1))
            bg = bm[r, gs]
            cg = cm[r, gs].astype(BF16)
            gram = _dot_nt(cg, bg.astype(BF16))
            for h in (2 * g, 2 * g + 1):
                seg = acs[:, HEAD_W * h:HEAD_W * h + 1] - acs_t[HEAD_W * h:HEAD_W * h + 1, :]
                decay = jnp.exp(jnp.where(tril, seg, -jnp.inf))
                yd = _dot((gram * decay).astype(BF16), dtx_b)
                y_diag = jnp.where(lane_head == h, yd, y_diag)
            prev = st_ref[g]
            y_off.append(_dot(cg, prev.astype(BF16)))
            st_ref[g] = prev * jnp.exp(a_last[:, gs]) + _dot(bg.T.astype(BF16), w_end[:, gs])
        y = y_diag + jnp.concatenate(y_off, axis=1) * jnp.exp(acs) + xs[r] * d_ref[...]
        y = y * _silu(z[r])
        o_ref[r, :] = _rms(y, g_ref[...]).astype(o_ref.dtype)


def _ssd(zx, dt, cw, cb, dtb, alog, dskip, g, b, s, ts):
    ns = s // ts
    flat = lambda width: pl.BlockSpec((ts, width), lambda bi, i: (bi * ns + i, 0))
    return pl.pallas_call(
        _ssd_kernel,
        grid=(b, ns),
        in_specs=[flat(IN_ZX), flat(IN_DT)] + [_resident(a.shape) for a in (cw, cb, dtb, alog, dskip, g)],
        out_specs=flat(GROUP_W),
        out_shape=jax.ShapeDtypeStruct((b * s, GROUP_W), BF16),
        scratch_shapes=[pltpu.VMEM((HALO + ts, SSD_XBC), F32), pltpu.VMEM((2, SSD_STATE, 2 * HEAD_W), F32)],
        compiler_params=_params(("parallel", "arbitrary")),
        name="ssd",
    )(zx, dt, cw, cb, dtb, alog, dskip, g)


def _ret_kernel(in_ref, cos_ref, sin_ref, gng_ref, gnb_ref, o_ref, st_ref):
    @pl.when(pl.program_id(1) == 0)
    def _():
        st_ref[...] = jnp.zeros(st_ref.shape, F32)

    xin = in_ref[...]
    ts = xin.shape[0]
    cos = cos_ref[...]
    sin = sin_ref[...]
    lane = lax.broadcasted_iota(jnp.int32, cos.shape, 1)
    first = (lane % HEAD_W) < (HEAD_W // 2)

    def rope(x):
        parts = []
        for half in range(GROUP_W // LANES):
            xh = x[:, LANES * half:LANES * (half + 1)]
            parts.append(xh * cos + _swap_halves(xh, first, HEAD_W // 2) * sin)
        return jnp.concatenate(parts, axis=1)

    q = rope(xin[:, :GROUP_W].astype(F32))
    k = rope(xin[:, GROUP_W:2 * GROUP_W].astype(F32)) * (HEAD_W ** -0.5)
    v = xin[:, 2 * GROUP_W:3 * GROUP_W]
    gate = xin[:, 3 * GROUP_W:].astype(F32)

    log_gamma = [math.log1p(-(2.0 ** (-5.0 - h))) for h in range(HEADS)]
    lane_head = lax.broadcasted_iota(jnp.int32, (CHUNK, GROUP_W), 1) // HEAD_W
    lg_lane = jnp.full((CHUNK, GROUP_W), log_gamma[HEADS - 1], F32)
    for h in range(HEADS - 1):
        lg_lane = jnp.where(lane_head == h, log_gamma[h], lg_lane)
    idx = lax.broadcasted_iota(jnp.int32, (CHUNK, GROUP_W), 0).astype(F32)
    q_dec = jnp.exp(lg_lane * (idx + 1.0))
    k_dec = jnp.exp(lg_lane * (CHUNK - 1.0 - idx))
    c_dec = jnp.exp(lg_lane[:1, :] * float(CHUNK))
    rel = (lax.broadcasted_iota(jnp.int32, (CHUNK, CHUNK), 0)
           - lax.broadcasted_iota(jnp.int32, (CHUNK, CHUNK), 1)).astype(F32)
    blk_r = lax.broadcasted_iota(jnp.int32, (GROUP_W, GROUP_W), 0) // HEAD_W
    blk_c = lax.broadcasted_iota(jnp.int32, (GROUP_W, GROUP_W), 1) // HEAD_W
    same_head = blk_r == blk_c
    avg = jnp.where(same_head, 1.0 / HEAD_W, 0.0).astype(BF16)

    for c in range(ts // CHUNK):
        r = slice(CHUNK * c, CHUNK * (c + 1))
        qc = q[r]
        kc = k[r]
        kb = kc.astype(BF16)
        vb = v[r]
        state = st_ref[...]
        o = _dot(qc.astype(BF16), state.astype(BF16)) * q_dec
        for h in range(HEADS):
            intra = jnp.where(rel >= 0.0, jnp.exp(log_gamma[h] * jnp.maximum(rel, 0.0)), 0.0)
            qh = jnp.where(lane_head == h, qc, 0.0).astype(BF16)
            sc = _dot_nt(qh, kb) * intra
            o = o + jnp.where(lane_head == h, _dot(sc.astype(BF16), vb), 0.0)
        upd = _dot((kc * k_dec).T.astype(BF16), vb)
        st_ref[...] = state * c_dec + jnp.where(same_head, upd, 0.0)
        mu = _dot_f32_lhs(o, avg)
        dlt = o - mu
        var = _dot_f32_lhs(dlt * dlt, avg)
        on = dlt * lax.rsqrt(var + NORM_EPS) * gng_ref[...] + gnb_ref[...]
        o_ref[r, :] = (_silu(gate[r]) * on).astype(o_ref.dtype)


def _retention(ret_in, cos_r, sin_r, gng, gnb, b, s, ts):
    ns = s // ts
    flat = lambda width: pl.BlockSpec((ts, width), lambda bi, i: (bi * ns + i, 0))
    return pl.pallas_call(
        _ret_kernel,
        grid=(b, ns),
        in_specs=[flat(IN_RET), flat(LANES), flat(LANES), _resident(gng.shape), _resident(gnb.shape)],
        out_specs=flat(GROUP_W),
        out_shape=jax.ShapeDtypeStruct((b * s, GROUP_W), BF16),
        scratch_shapes=[pltpu.VMEM((GROUP_W, GROUP_W), F32)],
        compiler_params=_params(("parallel", "arbitrary")),
        name="retention",
    )(ret_in, cos_r, sin_r, gng, gnb)


def _lru_kernel(in_ref, cw_ref, cb_ref, wa_ref, ba_ref, wx_ref, bx_ref, ap_ref, o_ref,
                halo_ref, h_ref):
    @pl.when(pl.program_id(1) == 0)
    def _():
        halo_ref[0:HALO, :] = jnp.zeros((HALO, halo_ref.shape[1]), F32)
        h_ref[...] = jnp.zeros(h_ref.shape, F32)

    xin = in_ref[...]
    ts = xin.shape[0]
    u = _causal_conv(xin[:, :GROUP_W].astype(F32), halo_ref, cw_ref[...], cb_ref[...])
    ub = u.astype(BF16)
    r = _sigmoid(_dot(ub, wa_ref[...]) + ba_ref[...])
    gate_in = _sigmoid(_dot(ub, wx_ref[...]) + bx_ref[...])
    log_a = -LRU_C * r * _softplus(-ap_ref[...])
    a = jnp.exp(log_a)
    b = jnp.sqrt(-jnp.tanh(log_a) * (a * a + 1.0)) * (gate_in * u)

    row = lax.broadcasted_iota(jnp.int32, a.shape, 0)
    stride = 1
    while stride < ts:
        live = row >= stride
        a_sh = jnp.where(live, pltpu.roll(a, stride, 0), 1.0)
        b_sh = jnp.where(live, pltpu.roll(b, stride, 0), 0.0)
        b = a * b_sh + b
        a = a * a_sh
        stride *= 2
    h = a * h_ref[0:1, :] + b
    h_ref[...] = jnp.broadcast_to(h[ts - 1:ts, :], h_ref.shape)
    o_ref[...] = (h * _gelu_tanh(xin[:, GROUP_W:].astype(F32))).astype(o_ref.dtype)


def _rglru(lru_in, cw, cb, wa, ba, wx, bx, ap, b, s, ts):
    ns = s // ts
    flat = lambda width: pl.BlockSpec((ts, width), lambda bi, i: (bi * ns + i, 0))
    return pl.pallas_call(
        _lru_kernel,
        grid=(b, ns),
        in_specs=[flat(IN_LRU)] + [_resident(a.shape) for a in (cw, cb, wa, ba, wx, bx, ap)],
        out_specs=flat(GROUP_W),
        out_shape=jax.ShapeDtypeStruct((b * s, GROUP_W), BF16),
        scratch_shapes=[pltpu.VMEM((HALO + ts, GROUP_W), F32), pltpu.VMEM((HALO, GROUP_W), F32)],
        compiler_params=_params(("parallel", "arbitrary")),
        name="rglru",
    )(lru_in, cw, cb, wa, ba, wx, bx, ap)


def _out_ln_kernel(ya_ref, yb_ref, yc_ref, yd_ref, x_ref, w_ref, g_ref, b_ref, o_ref, *, alpha):
    ya = jnp.concatenate([ya_ref[0, h] for h in range(HEADS)], axis=1)
    mix = _dot(ya, w_ref[0:GROUP_W, :])
    for n, ref in enumerate((yb_ref, yc_ref, yd_ref), start=1):
        mix = mix + _dot(ref[...], w_ref[GROUP_W * n:GROUP_W * (n + 1), :])
    o_ref[...] = _layernorm(alpha * x_ref[...] + mix, g_ref[...], b_ref[...])


def _out_ln(ys, x, w, g, bias, alpha, tm):
    t, d = x.shape
    ns = ys[0].shape[2] // tm
    atile = pl.BlockSpec((1, HEADS, tm, HEAD_W), lambda i: (i // ns, 0, i % ns, 0))
    ytile = pl.BlockSpec((tm, GROUP_W), lambda i: (i, 0))
    xtile = pl.BlockSpec((tm, d), lambda i: (i, 0))
    return pl.pallas_call(
        functools.partial(_out_ln_kernel, alpha=alpha),
        grid=(t // tm,),
        in_specs=[atile] + [ytile] * 3 + [xtile, _resident(w.shape), _resident(g.shape), _resident(bias.shape)],
        out_specs=xtile,
        out_shape=jax.ShapeDtypeStruct((t, d), F32),
        compiler_params=_params(("parallel",)),
        name="out_proj_ln",
    )(*ys, x, w, g, bias)


FFN_CHUNK = 256


def _ffn_ln_kernel(x_ref, wi_ref, wo_ref, g_ref, b_ref, o_ref, *, alpha):
    x = x_ref[...]
    xb = x.astype(BF16)
    d_ff = wo_ref.shape[0]
    acc = jnp.zeros(x.shape, F32)
    for c in range(d_ff // FFN_CHUNK):
        lo = FFN_CHUNK * c
        gte = _dot(xb, wi_ref[:, lo:lo + FFN_CHUNK])
        up = _dot(xb, wi_ref[:, d_ff + lo:d_ff + lo + FFN_CHUNK])
        acc = acc + _dot((_silu(gte) * up).astype(BF16), wo_ref[lo:lo + FFN_CHUNK, :])
    o_ref[...] = _layernorm(alpha * x + acc, g_ref[...], b_ref[...])


def _ffn_ln(x, wi, wo, g, bias, alpha, tm):
    t, d = x.shape
    assert wo.shape[0] % FFN_CHUNK == 0
    xtile = pl.BlockSpec((tm, d), lambda i: (i, 0))
    return pl.pallas_call(
        functools.partial(_ffn_ln_kernel, alpha=alpha),
        grid=(t // tm,),
        in_specs=[xtile, _resident(wi.shape), _resident(wo.shape), _resident(g.shape), _resident(bias.shape)],
        out_specs=xtile,
        out_shape=jax.ShapeDtypeStruct((t, d), F32),
        compiler_params=_params(("parallel",)),
        name="ffn_ln",
    )(x, wi, wo, g, bias)


SRC_KR = GROUP_W + MLA_KV_LORA
SRC_ZX = SRC_KR + MLA_ROPE
SRC_DT = SRC_ZX + IN_ZX
SRC_REST = SRC_DT + HEADS
N_IN = SRC_REST + IN_RET + IN_LRU


def _pack_in_proj_kernel(w_ref, o_ref):
    rows = o_ref.shape[0]
    o_ref[:, 0:SRC_KR] = w_ref[0, :, 0:SRC_KR].astype(BF16)
    kr = w_ref[0, :, SRC_KR:SRC_ZX].astype(BF16)
    o_ref[:, SRC_KR:IN_MLA] = jnp.concatenate(
        [kr, jnp.zeros((rows, IN_MLA - SRC_ZX), BF16)], axis=1)
    o_ref[:, IN_MLA:IN_MLA + IN_ZX] = w_ref[0, :, SRC_ZX:SRC_DT].astype(BF16)
    dt0 = IN_MLA + IN_ZX
    for h in range(HEADS):
        col = w_ref[0, :, SRC_DT + h:SRC_DT + h + 1]
        o_ref[:, dt0 + HEAD_W * h:dt0 + HEAD_W * (h + 1)] = jnp.broadcast_to(col, (rows, HEAD_W)).astype(BF16)
    o_ref[:, dt0 + IN_DT:IN_PAD] = w_ref[0, :, SRC_REST:N_IN].astype(BF16)


def _pack_in_proj(w_in, layer):
    _, d, n = w_in.shape
    assert n == N_IN
    return pl.pallas_call(
        _pack_in_proj_kernel,
        grid=(1,),
        in_specs=[pl.BlockSpec((1, d, n), lambda i: (layer, 0, 0))],
        out_specs=pl.BlockSpec((d, IN_PAD), lambda i: (0, 0)),
        out_shape=jax.ShapeDtypeStruct((d, IN_PAD), BF16),
        compiler_params=_params(("arbitrary",)),
        name="pack_in_proj",
    )(w_in)


def _per_head_lanes(w, width):
    k = w.shape[0]
    w = w.reshape(k, HEADS, width)
    return jnp.pad(w, ((0, 0), (0, 0), (0, LANES - width))).reshape(k, HEADS * LANES)


def _row(v):
    return v.reshape(1, -1).astype(F32)


def _head_row(v):
    return jnp.repeat(v.astype(F32), HEAD_W).reshape(1, GROUP_W)


def _block_diag(w):
    nb, n, _ = w.shape
    eye = jnp.eye(nb, dtype=w.dtype)
    return jnp.einsum("gij,gh->gihj", w, eye).reshape(nb * n, nb * n)


def _rope_rows():
    inv_m = ROPE_THETA ** (-jnp.arange(0, MLA_ROPE, 2, dtype=F32) / MLA_ROPE)
    inv_r = ROPE_THETA ** (-jnp.arange(0, HEAD_W, 2, dtype=F32) / HEAD_W)
    zeros = lambda n: jnp.zeros((n,), F32)
    ones = lambda n: jnp.ones((n,), F32)
    fm = jnp.concatenate([zeros(MLA_NOPE), inv_m, inv_m, zeros(HEAD_W - MLA_QK)])
    sm = jnp.concatenate([zeros(MLA_NOPE), -ones(8), ones(8), zeros(HEAD_W - MLA_QK)])
    fm = jnp.concatenate([fm, zeros(LANES - HEAD_W)])
    sm = jnp.concatenate([sm, zeros(LANES - HEAD_W)])
    fr = jnp.tile(jnp.concatenate([inv_r, inv_r]), LANES // HEAD_W)
    sr = jnp.tile(jnp.concatenate([-ones(HEAD_W // 2), ones(HEAD_W // 2)]), LANES // HEAD_W)
    return [a.reshape(1, LANES) for a in (fm, sm, fr, sr)]


def _forward(x, positions, p, tiles):
    b, s, d = x.shape
    t = b * s
    depth = p["w_in"].shape[0]
    alpha = (2.0 * depth) ** 0.25
    xf = x.reshape(t, d)
    pos = positions.reshape(t, 1).astype(F32)
    cos_m, sin_m, cos_r, sin_r = _rope_tables(pos, *_rope_rows(), tiles["rope"])

    e_mat = np.zeros((LANES, HEADS * LANES), np.float32)
    vone = np.zeros((1, HEADS * LANES), np.float32)
    for h in range(HEADS):
        for j in range(MLA_ROPE):
            e_mat[j, LANES * h + MLA_NOPE + j] = 1.0
        vone[0, LANES * h + HEAD_W] = 1.0
    e_mat = jnp.asarray(e_mat, BF16)
    vone = jnp.asarray(vone, F32)

    for l in range(depth):
        mla_in, zx, dt, ret_in, lru_in = _in_proj(xf, _pack_in_proj(p["w_in"], l), tiles["proj"])

        w_ukv = p["mla_w_ukv"][l].reshape(MLA_KV_LORA, HEADS, MLA_NOPE + HEAD_W)
        wq = _per_head_lanes(p["mla_w_uq"][l], MLA_QK).astype(BF16)
        wk = _per_head_lanes(w_ukv[:, :, :MLA_NOPE].reshape(MLA_KV_LORA, -1), MLA_NOPE).astype(BF16)
        wv = _per_head_lanes(w_ukv[:, :, MLA_NOPE:].reshape(MLA_KV_LORA, -1), HEAD_W).astype(BF16)
        q, k, v = _mla_prep(mla_in, cos_m, sin_m, _row(p["mla_g_q"][l]), _row(p["mla_g_kv"][l]),
                            wq, wk, wv, e_mat, vone, b, s, tiles["attn"])
        y_a = _attention(q, k, v, tiles["attn"])

        y_b = _ssd(zx, dt, p["ssd_conv_w"][l].astype(F32), _row(p["ssd_conv_b"][l]),
                   _head_row(p["ssd_dt_bias"][l]), _head_row(p["ssd_a_log"][l]), _head_row(p["ssd_d"][l]),
                   _row(p["ssd_norm_g"][l]), b, s, tiles["seq"])
        y_c = _retention(ret_in, cos_r, sin_r, _row(p["ret_gn_g"][l]), _row(p["ret_gn_b"][l]),
                         b, s, tiles["seq"])
        y_d = _rglru(lru_in, p["lru_conv_w"][l].astype(F32), _row(p["lru_conv_b"][l]),
                     _block_diag(p["lru_w_a"][l]).astype(BF16), _row(p["lru_b_a"][l]),
                     _block_diag(p["lru_w_x"][l]).astype(BF16), _row(p["lru_b_x"][l]),
                     _row(p["lru_a_param"][l]), b, s, tiles["seq"])

        x1 = _out_ln((y_a, y_b, y_c, y_d), xf, p["w_out"][l].astype(BF16), _row(p["ln1_g"][l]),
                     _row(p["ln1_b"][l]), alpha, tiles["proj"])
        xf = _ffn_ln(x1, p["w_ffn_in"][l].astype(BF16), p["w_ffn_out"][l].astype(BF16),
                     _row(p["ln2_g"][l]), _row(p["ln2_b"][l]), alpha, tiles["proj"])
    return xf.reshape(b, s, d)


def _tiles(s):
    return {"rope": min(1024, s), "proj": min(512, s), "attn": min(1024, s), "seq": min(256, s)}


def kernel(x, positions, w_in, mla_g_q, mla_w_uq, mla_g_kv, mla_w_ukv, ssd_conv_w, ssd_conv_b, ssd_dt_bias, ssd_a_log, ssd_d, ssd_norm_g, ret_gn_g, ret_gn_b, lru_conv_w, lru_conv_b, lru_w_a, lru_b_a, lru_w_x, lru_b_x, lru_a_param, w_out, ln1_g, ln1_b, w_ffn_in, w_ffn_out, ln2_g, ln2_b):
    p = dict(w_in=w_in, mla_g_q=mla_g_q, mla_w_uq=mla_w_uq, mla_g_kv=mla_g_kv, mla_w_ukv=mla_w_ukv,
             ssd_conv_w=ssd_conv_w, ssd_conv_b=ssd_conv_b, ssd_dt_bias=ssd_dt_bias, ssd_a_log=ssd_a_log,
             ssd_d=ssd_d, ssd_norm_g=ssd_norm_g, ret_gn_g=ret_gn_g, ret_gn_b=ret_gn_b,
             lru_conv_w=lru_conv_w, lru_conv_b=lru_conv_b, lru_w_a=lru_w_a, lru_b_a=lru_b_a,
             lru_w_x=lru_w_x, lru_b_x=lru_b_x, lru_a_param=lru_a_param, w_out=w_out, ln1_g=ln1_g,
             ln1_b=ln1_b, w_ffn_in=w_ffn_in, w_ffn_out=w_ffn_out, ln2_g=ln2_g, ln2_b=ln2_b)
    return _forward(x, positions, p, _tiles(x.shape[1]))
```

```python
import functools
import math

import jax
import jax.numpy as jnp
import numpy as np
from jax import lax
from jax.experimental import pallas as pl
from jax.experimental.pallas import tpu as pltpu

F32 = jnp.float32
BF16 = jnp.bfloat16

GROUP_W = 256
HEADS = 4
HEAD_W = 64
MLA_NOPE = 32
MLA_ROPE = 16
MLA_QK = MLA_NOPE + MLA_ROPE
MLA_KV_LORA = 128
SSD_STATE = 128
SSD_XBC = 768
CHUNK = 128
CONV_K = 4
LRU_C = 8.0
ROPE_THETA = 10000.0
NORM_EPS = 1e-5
HALO = 8
VT_ROWS = 80
ATTN_CHUNK = 256

LANES = 128
VMEM_LIMIT = 56 * 1024 * 1024

IN_MLA = 512
IN_ZX = 1024
IN_DT = 256
IN_RET = 1024
IN_LRU = 512
IN_PAD = IN_MLA + IN_ZX + IN_DT + IN_RET + IN_LRU


def _sigmoid(x):
    return 1.0 / (1.0 + jnp.exp(-x))


def _silu(x):
    return x * _sigmoid(x)


def _softplus(x):
    return jnp.maximum(x, 0.0) + jnp.log1p(jnp.exp(-jnp.abs(x)))


def _gelu_tanh(x):
    c = math.sqrt(2.0 / math.pi)
    return x * (0.5 * (1.0 + jnp.tanh(c * (x + 0.044715 * (x * x * x)))))


def _rms(x, g):
    return x * lax.rsqrt(jnp.mean(x * x, axis=-1, keepdims=True) + NORM_EPS) * g


def _layernorm(v, g, b):
    mu = jnp.mean(v, axis=-1, keepdims=True)
    d = v - mu
    var = jnp.mean(d * d, axis=-1, keepdims=True)
    return d * lax.rsqrt(var + NORM_EPS) * g + b


def _dot(a, b):
    return jnp.dot(a, b, preferred_element_type=F32)


def _dot_nt(a, b):
    return lax.dot_general(a, b, (((1,), (1,)), ((), ())), preferred_element_type=F32)


def _split3(x):
    hi = x.astype(BF16)
    rest = x - hi.astype(F32)
    mid = rest.astype(BF16)
    lo = (rest - mid.astype(F32)).astype(BF16)
    return hi, mid, lo


def _dot_f32_rhs(c, x):
    return sum(_dot(c, part) for part in _split3(x))


def _dot_f32_lhs(x, c):
    return sum(_dot(part, c) for part in _split3(x))


def _resident(shape):
    nd = len(shape)
    return pl.BlockSpec(shape, lambda *_: (0,) * nd, pipeline_mode=pl.Buffered(1))


def _params(sem):
    return pltpu.CompilerParams(dimension_semantics=sem, vmem_limit_bytes=VMEM_LIMIT)


def _rope_tab_kernel(pos_ref, f_ref, s_ref, cm_ref, snm_ref, cr_ref, snr_ref):
    ang = pos_ref[...] * f_ref[...]
    cos = jnp.cos(ang)
    sin = jnp.sin(ang) * s_ref[...]
    cos_sw = pltpu.roll(cos, HEAD_W, 1)
    sin_sw = pltpu.roll(sin, HEAD_W, 1)
    cm_ref[...] = cos_sw
    snm_ref[...] = sin_sw
    lower = lax.broadcasted_iota(jnp.int32, cos.shape, 1) < HEAD_W
    cr_ref[...] = jnp.where(lower, cos, cos_sw)
    snr_ref[...] = jnp.where(lower, sin, sin_sw)


def _rope_tables(pos, freq, sign, tm):
    t = pos.shape[0]
    row = pl.BlockSpec((1, LANES), lambda i: (0, 0))
    tab = pl.BlockSpec((tm, LANES), lambda i: (i, 0))
    return pl.pallas_call(
        _rope_tab_kernel,
        grid=(t // tm,),
        in_specs=[pl.BlockSpec((tm, 1), lambda i: (i, 0)), row, row],
        out_specs=[tab, tab, tab, tab],
        out_shape=[jax.ShapeDtypeStruct((t, LANES), F32)] * 4,
        compiler_params=_params(("parallel",)),
        name="rope_tables",
    )(pos, freq, sign)


def _in_proj_kernel(x_ref, w_ref, mla_ref, zx_ref, dt_ref, ret_ref, lru_ref):
    xb = x_ref[...].astype(BF16)
    start = 0
    for ref in (mla_ref, zx_ref, dt_ref, ret_ref, lru_ref):
        width = ref.shape[-1]
        ref[...] = _dot(xb, w_ref[:, start:start + width]).astype(ref.dtype)
        start += width


def _in_proj(x, w_pad, tm):
    t, d = x.shape
    widths = (IN_MLA, IN_ZX, IN_DT, IN_RET, IN_LRU)
    dtypes = (BF16, BF16, F32, BF16, BF16)
    return pl.pallas_call(
        _in_proj_kernel,
        grid=(t // tm,),
        in_specs=[pl.BlockSpec((tm, d), lambda i: (i, 0)), _resident(w_pad.shape)],
        out_specs=[pl.BlockSpec((tm, w), lambda i: (i, 0)) for w in widths],
        out_shape=[jax.ShapeDtypeStruct((t, w), dt) for w, dt in zip(widths, dtypes)],
        compiler_params=_params(("parallel",)),
        name="in_proj",
    )(x, w_pad)


def _swap_halves(x, lane_in_group, half):
    first = lane_in_group
    return jnp.where(first, pltpu.roll(x, LANES - half, 1), pltpu.roll(x, half, 1))


def _mla_prep_kernel(in_ref, cos_ref, sin_ref, gq_ref, gkv_ref, wq_ref, wk_ref, wv_ref, e_ref,
                     vone_ref, qt_out, k_out, vt_out):
    xin = in_ref[...]
    cq = xin[:, :GROUP_W].astype(F32)
    ckv = xin[:, GROUP_W:GROUP_W + MLA_KV_LORA].astype(F32)
    kr = xin[:, GROUP_W + MLA_KV_LORA:]
    nq = _rms(cq, gq_ref[...]).astype(BF16)
    nkv = _rms(ckv, gkv_ref[...]).astype(BF16)
    q = _dot(nq, wq_ref[...])
    k = _dot(nkv, wk_ref[...]) + _dot(kr, e_ref[...])
    v = _dot(nkv, wv_ref[...]) + vone_ref[...]
    cos = cos_ref[...]
    sin = sin_ref[...]
    lane = lax.broadcasted_iota(jnp.int32, cos.shape, 1)
    first = (lane % HEAD_W) < (MLA_NOPE + MLA_ROPE // 2)
    scale = MLA_QK ** -0.5 * math.log2(math.e)
    for h in range(HEADS):
        sl = slice(LANES * h, LANES * (h + 1))
        qh = q[:, sl]
        kh = k[:, sl]
        qh = qh * cos + _swap_halves(qh, first, MLA_ROPE // 2) * sin
        kh = kh * cos + _swap_halves(kh, first, MLA_ROPE // 2) * sin
        qt_out[0, h, 0] = (qh * scale).T.astype(BF16)
        k_out[0, h] = kh.astype(BF16)
        vt_out[0, h, 0] = v[:, sl].T[:VT_ROWS, :].astype(BF16)


def _mla_prep(mla_in, cos_m, sin_m, gq, gkv, wq, wk, wv, e_mat, vone, b, s, tm):
    ns = s // tm
    flat = lambda width: pl.BlockSpec((tm, width), lambda bi, i: (bi * ns + i, 0))
    out = pl.BlockSpec((1, HEADS, tm, LANES), lambda bi, i: (bi, 0, i, 0))
    out_t = lambda rows: pl.BlockSpec((1, HEADS, 1, rows, tm), lambda bi, i: (bi, 0, i, 0, 0))
    t_shape = lambda rows: jax.ShapeDtypeStruct((b, HEADS, ns, rows, tm), BF16)
    return pl.pallas_call(
        _mla_prep_kernel,
        grid=(b, ns),
        in_specs=[flat(IN_MLA), flat(LANES), flat(LANES), _resident(gq.shape), _resident(gkv.shape),
                  _resident(wq.shape), _resident(wk.shape), _resident(wv.shape),
                  _resident(e_mat.shape), _resident(vone.shape)],
        out_specs=[out_t(LANES), out, out_t(VT_ROWS)],
        out_shape=[t_shape(LANES), jax.ShapeDtypeStruct((b, HEADS, s, LANES), BF16), t_shape(VT_ROWS)],
        compiler_params=_params(("parallel", "parallel")),
        name="mla_prep",
    )(mla_in, cos_m, sin_m, gq, gkv, wq, wk, wv, e_mat, vone)


def _attn_kernel(qt_ref, k_ref, vt_ref, o_ref, sa_ref, sb_ref, sc_ref, mca_ref, mcb_ref, mcc_ref,
                 m_ref, acc_ref, *, t):
    nt = k_ref.shape[2] // t
    ch = min(ATTN_CHUNK, t)

    def reset():
        m_ref[...] = jnp.full(m_ref.shape, -jnp.inf, F32)
        acc_ref[...] = jnp.zeros(acc_ref.shape, F32)

    def step(fill=None, drain=None):
        if fill is not None:
            fs_ref, fmc_ref, f_qt, f_kt, f_diag = fill
            qt = qt_ref[0, 0, f_qt]
        if drain is not None:
            ds_ref, dmc_ref, d_kt, d_diag = drain
            m_prev = m_ref[0:1, :]
            m_new = jnp.maximum(m_prev, dmc_ref[0:1, :])
            alpha = jnp.exp2(m_prev - m_new)
        for c in range(t // ch):
            cols = slice(ch * c, ch * (c + 1))
            if fill is not None:
                nk = ch * (c + 1) if f_diag else t
                kt = k_ref[0, 0, pl.ds(pl.multiple_of(f_kt * t, t), nk), :]
                st = _dot(kt, qt[:, cols])
                if f_diag:
                    key = lax.broadcasted_iota(jnp.int32, st.shape, 0)
                    qry = lax.broadcasted_iota(jnp.int32, st.shape, 1) + ch * c
                    st = jnp.where(qry >= key, st, -jnp.inf)
                fs_ref[0:nk, cols] = st
                fmc_ref[:, cols] = jnp.broadcast_to(jnp.max(st, axis=0, keepdims=True), (HALO, ch))
            if drain is not None:
                nk = ch * (c + 1) if d_diag else t
                pt = jnp.exp2(ds_ref[0:nk, cols] - m_new[:, cols]).astype(BF16)
                pv = _dot(vt_ref[0, 0, d_kt, :, 0:nk], pt)
                acc_ref[:, cols] = alpha[:, cols] * acc_ref[:, cols] + pv
        if drain is not None:
            m_ref[...] = jnp.broadcast_to(m_new, m_ref.shape)

    def finish(qi):
        acc = acc_ref[...]
        out_t = acc[:HEAD_W, :] / acc[HEAD_W:HEAD_W + 1, :]
        o_ref[0, 0, pl.ds(pl.multiple_of(qi * t, t), t), :] = out_t.T.astype(o_ref.dtype)
        reset()

    reset()
    step(fill=(sc_ref, mcc_ref, 0, 0, True))
    step(drain=(sc_ref, mcc_ref, 0, True))
    finish(0)
    if nt > 1:
        step(fill=(sc_ref, mcc_ref, 1, 1, True))

    def query_tile(qi, carry):
        nxt = jnp.minimum(qi + 1, nt - 1)
        a, b, c = (sa_ref, mca_ref), (sb_ref, mcb_ref), (sc_ref, mcc_ref)
        step(fill=a + (qi, 0, False), drain=c + (qi, True))

        def pair(i, c2):
            step(fill=b + (qi, 2 * i + 1, False), drain=a + (2 * i, False))
            step(fill=a + (qi, 2 * i + 2, False), drain=b + (2 * i + 1, False))
            return c2

        lax.fori_loop(0, (qi - 1) // 2, pair, 0)

        @pl.when(qi % 2 == 1)
        def _():
            step(fill=c + (nxt, nxt, True), drain=a + (qi - 1, False))
            finish(qi)

        @pl.when(qi % 2 == 0)
        def _():
            step(fill=b + (qi, qi - 1, False), drain=a + (qi - 2, False))
            step(fill=c + (nxt, nxt, True), drain=b + (qi - 1, False))
            finish(qi)

        return carry

    lax.fori_loop(1, nt, query_tile, 0)


def _attention(qt, k, vt, t):
    b, heads, s, _ = k.shape
    nt = s // t
    assert qt.shape == (b, heads, nt, LANES, t) and vt.shape == (b, heads, nt, VT_ROWS, t)
    tiles = lambda rows: pl.BlockSpec((1, 1, nt, rows, t), lambda bi, h: (bi, h, 0, 0, 0),
                                      pipeline_mode=pl.Buffered(1))
    return pl.pallas_call(
        functools.partial(_attn_kernel, t=t),
        grid=(b, heads),
        in_specs=[tiles(LANES),
                  pl.BlockSpec((1, 1, s, LANES), lambda bi, h: (bi, h, 0, 0), pipeline_mode=pl.Buffered(1)),
                  tiles(VT_ROWS)],
        out_specs=pl.BlockSpec((1, 1, s, HEAD_W), lambda bi, h: (bi, h, 0, 0)),
        out_shape=jax.ShapeDtypeStruct((b, heads, s, HEAD_W), BF16),
        scratch_shapes=[pltpu.VMEM((t, t), F32)] * 3 + [pltpu.VMEM((HALO, t), F32)] * 4
                       + [pltpu.VMEM((VT_ROWS, t), F32)],
        compiler_params=_params(("parallel", "parallel")),
        name="mla_attention",
    )(qt, k, vt)


def _causal_conv(x, xe_ref, w, bias):
    ts = x.shape[0]
    xe_ref[HALO:, :] = x
    y = w[CONV_K - 1:CONV_K, :] * x + bias
    for back in range(1, CONV_K):
        tap = w[CONV_K - 1 - back:CONV_K - back, :]
        y = y + tap * xe_ref[pl.ds(HALO - back, ts), :]
    xe_ref[0:HALO, :] = x[ts - HALO:, :]
    return y


def _ssd_kernel(zx_ref, dt_ref, cw_ref, cb_ref, dtb_ref, alog_ref, d_ref, g_ref, o_ref,
                halo_ref, st_ref):
    @pl.when(pl.program_id(1) == 0)
    def _():
        halo_ref[0:HALO, :] = jnp.zeros((HALO, halo_ref.shape[1]), F32)
        st_ref[...] = jnp.zeros(st_ref.shape, F32)

    zx = zx_ref[...]
    ts = zx.shape[0]
    z = zx[:, :GROUP_W].astype(F32)
    xbc = _silu(_causal_conv(zx[:, GROUP_W:].astype(F32), halo_ref, cw_ref[...], cb_ref[...]))
    xs = xbc[:, :GROUP_W]
    bm = xbc[:, GROUP_W:2 * GROUP_W]
    cm = xbc[:, 2 * GROUP_W:]
    dt = _softplus(dt_ref[...] + dtb_ref[...])
    da = dt * (-jnp.exp(alog_ref[...]))
    dtx = dt * xs

    row = lax.broadcasted_iota(jnp.int32, (CHUNK, CHUNK), 0)
    col = lax.broadcasted_iota(jnp.int32, (CHUNK, CHUNK), 1)
    tril = row >= col
    tril_b = tril.astype(BF16)
    lane_head = lax.broadcasted_iota(jnp.int32, (CHUNK, GROUP_W), 1) // HEAD_W

    for c in range(ts // CHUNK):
        r = slice(CHUNK * c, CHUNK * (c + 1))
        acs = _dot_f32_rhs(tril_b, da[r])
        acs_t = acs.T
        a_last = acs[CHUNK - 1:CHUNK, :]
        dtx_c = dtx[r]
        dtx_b = dtx_c.astype(BF16)
        w_end = (jnp.exp(a_last - acs) * dtx_c).astype(BF16)
        y_diag = jnp.zeros((CHUNK, GROUP_W), F32)
        y_off = []
        for g in range(2):
            gs = slice(SSD_STATE * g, SSD_STATE * (g + 1))
            bg = bm[r, gs]
            cg = cm[r, gs].astype(BF16)
            gram = _dot_nt(cg, bg.astype(BF16))
            for h in (2 * g, 2 * g + 1):
                seg = acs[:, HEAD_W * h:HEAD_W * h + 1] - acs_t[HEAD_W * h:HEAD_W * h + 1, :]
                decay = jnp.exp(jnp.where(tril, seg, -jnp.inf))
                yd = _dot((gram * decay).astype(BF16), dtx_b)
                y_diag = jnp.where(lane_head == h, yd, y_diag)
            prev = st_ref[g]
            y_off.append(_dot(cg, prev.astype(BF16)))
            st_ref[g] = prev * jnp.exp(a_last[:, gs]) + _dot(bg.T.astype(BF16), w_end[:, gs])
        y = y_diag + jnp.concatenate(y_off, axis=1) * jnp.exp(acs) + xs[r] * d_ref[...]
        y = y * _silu(z[r])
        o_ref[r, :] = _rms(y, g_ref[...]).astype(o_ref.dtype)


def _ssd(zx, dt, cw, cb, dtb, alog, dskip, g, b, s, ts):
    ns = s // ts
    flat = lambda width: pl.BlockSpec((ts, width), lambda bi, i: (bi * ns + i, 0))
    return pl.pallas_call(
        _ssd_kernel,
        grid=(b, ns),
        in_specs=[flat(IN_ZX), flat(IN_DT)] + [_resident(a.shape) for a in (cw, cb, dtb, alog, dskip, g)],
        out_specs=flat(GROUP_W),
        out_shape=jax.ShapeDtypeStruct((b * s, GROUP_W), BF16),
        scratch_shapes=[pltpu.VMEM((HALO + ts, SSD_XBC), F32), pltpu.VMEM((2, SSD_STATE, 2 * HEAD_W), F32)],
        compiler_params=_params(("parallel", "arbitrary")),
        name="ssd",
    )(zx, dt, cw, cb, dtb, alog, dskip, g)


def _ret_kernel(in_ref, cos_ref, sin_ref, gng_ref, gnb_ref, o_ref, st_ref):
    @pl.when(pl.program_id(1) == 0)
    def _():
        st_ref[...] = jnp.zeros(st_ref.shape, F32)

    xin = in_ref[...]
    ts = xin.shape[0]
    cos = cos_ref[...]
    sin = sin_ref[...]
    lane = lax.broadcasted_iota(jnp.int32, cos.shape, 1)
    first = (lane % HEAD_W) < (HEAD_W // 2)

    def rope(x):
        parts = []
        for half in range(GROUP_W // LANES):
            xh = x[:, LANES * half:LANES * (half + 1)]
            parts.append(xh * cos + _swap_halves(xh, first, HEAD_W // 2) * sin)
        return jnp.concatenate(parts, axis=1)

    q = rope(xin[:, :GROUP_W].astype(F32))
    k = rope(xin[:, GROUP_W:2 * GROUP_W].astype(F32)) * (HEAD_W ** -0.5)
    v = xin[:, 2 * GROUP_W:3 * GROUP_W]
    gate = xin[:, 3 * GROUP_W:].astype(F32)

    log_gamma = [math.log1p(-(2.0 ** (-5.0 - h))) for h in range(HEADS)]
    lane_head = lax.broadcasted_iota(jnp.int32, (CHUNK, GROUP_W), 1) // HEAD_W
    lg_lane = jnp.full((CHUNK, GROUP_W), log_gamma[HEADS - 1], F32)
    for h in range(HEADS - 1):
        lg_lane = jnp.where(lane_head == h, log_gamma[h], lg_lane)
    idx = lax.broadcasted_iota(jnp.int32, (CHUNK, GROUP_W), 0).astype(F32)
    q_dec = jnp.exp(lg_lane * (idx + 1.0))
    k_dec = jnp.exp(lg_lane * (CHUNK - 1.0 - idx))
    c_dec = jnp.exp(lg_lane[:1, :] * float(CHUNK))
    rel = (lax.broadcasted_iota(jnp.int32, (CHUNK, CHUNK), 0)
           - lax.broadcasted_iota(jnp.int32, (CHUNK, CHUNK), 1)).astype(F32)
    blk_r = lax.broadcasted_iota(jnp.int32, (GROUP_W, GROUP_W), 0) // HEAD_W
    blk_c = lax.broadcasted_iota(jnp.int32, (GROUP_W, GROUP_W), 1) // HEAD_W
    same_head = blk_r == blk_c
    avg = jnp.where(same_head, 1.0 / HEAD_W, 0.0).astype(BF16)

    for c in range(ts // CHUNK):
        r = slice(CHUNK * c, CHUNK * (c + 1))
        qc = q[r]
        kc = k[r]
        kb = kc.astype(BF16)
        vb = v[r]
        state = st_ref[...]
        o = _dot(qc.astype(BF16), state.astype(BF16)) * q_dec
        for h in range(HEADS):
            intra = jnp.where(rel >= 0.0, jnp.exp(log_gamma[h] * jnp.maximum(rel, 0.0)), 0.0)
            qh = jnp.where(lane_head == h, qc, 0.0).astype(BF16)
            sc = _dot_nt(qh, kb) * intra
            o = o + jnp.where(lane_head == h, _dot(sc.astype(BF16), vb), 0.0)
        upd = _dot((kc * k_dec).T.astype(BF16), vb)
        st_ref[...] = state * c_dec + jnp.where(same_head, upd, 0.0)
        mu = _dot_f32_lhs(o, avg)
        dlt = o - mu
        var = _dot_f32_lhs(dlt * dlt, avg)
        on = dlt * lax.rsqrt(var + NORM_EPS) * gng_ref[...] + gnb_ref[...]
        o_ref[r, :] = (_silu(gate[r]) * on).astype(o_ref.dtype)


def _retention(ret_in, cos_r, sin_r, gng, gnb, b, s, ts):
    ns = s // ts
    flat = lambda width: pl.BlockSpec((ts, width), lambda bi, i: (bi * ns + i, 0))
    return pl.pallas_call(
        _ret_kernel,
        grid=(b, ns),
        in_specs=[flat(IN_RET), flat(LANES), flat(LANES), _resident(gng.shape), _resident(gnb.shape)],
        out_specs=flat(GROUP_W),
        out_shape=jax.ShapeDtypeStruct((b * s, GROUP_W), BF16),
        scratch_shapes=[pltpu.VMEM((GROUP_W, GROUP_W), F32)],
        compiler_params=_params(("parallel", "arbitrary")),
        name="retention",
    )(ret_in, cos_r, sin_r, gng, gnb)


def _lru_kernel(in_ref, cw_ref, cb_ref, wa_ref, ba_ref, wx_ref, bx_ref, ap_ref, o_ref,
                halo_ref, h_ref):
    @pl.when(pl.program_id(1) == 0)
    def _():
        halo_ref[0:HALO, :] = jnp.zeros((HALO, halo_ref.shape[1]), F32)
        h_ref[...] = jnp.zeros(h_ref.shape, F32)

    xin = in_ref[...]
    ts = xin.shape[0]
    u = _causal_conv(xin[:, :GROUP_W].astype(F32), halo_ref, cw_ref[...], cb_ref[...])
    ub = u.astype(BF16)
    r = _sigmoid(_dot(ub, wa_ref[...]) + ba_ref[...])
    gate_in = _sigmoid(_dot(ub, wx_ref[...]) + bx_ref[...])
    log_a = -LRU_C * r * _softplus(-ap_ref[...])
    a = jnp.exp(log_a)
    b = jnp.sqrt(-jnp.tanh(log_a) * (a * a + 1.0)) * (gate_in * u)

    row = lax.broadcasted_iota(jnp.int32, a.shape, 0)
    stride = 1
    while stride < ts:
        live = row >= stride
        a_sh = jnp.where(live, pltpu.roll(a, stride, 0), 1.0)
        b_sh = jnp.where(live, pltpu.roll(b, stride, 0), 0.0)
        b = a * b_sh + b
        a = a * a_sh
        stride *= 2
    h = a * h_ref[0:1, :] + b
    h_ref[...] = jnp.broadcast_to(h[ts - 1:ts, :], h_ref.shape)
    o_ref[...] = (h * _gelu_tanh(xin[:, GROUP_W:].astype(F32))).astype(o_ref.dtype)


def _rglru(lru_in, cw, cb, wa, ba, wx, bx, ap, b, s, ts):
    ns = s // ts
    flat = lambda width: pl.BlockSpec((ts, width), lambda bi, i: (bi * ns + i, 0))
    return pl.pallas_call(
        _lru_kernel,
        grid=(b, ns),
        in_specs=[flat(IN_LRU)] + [_resident(a.shape) for a in (cw, cb, wa, ba, wx, bx, ap)],
        out_specs=flat(GROUP_W),
        out_shape=jax.ShapeDtypeStruct((b * s, GROUP_W), BF16),
        scratch_shapes=[pltpu.VMEM((HALO + ts, GROUP_W), F32), pltpu.VMEM((HALO, GROUP_W), F32)],
        compiler_params=_params(("parallel", "arbitrary")),
        name="rglru",
    )(lru_in, cw, cb, wa, ba, wx, bx, ap)


FFN_CHUNK = 256


def _mix_ffn_kernel(ya_ref, yb_ref, yc_ref, yd_ref, x_ref, w_ref, g1_ref, b1_ref, wi_ref, wo_ref,
                    g2_ref, b2_ref, o_ref, *, alpha):
    ya = jnp.concatenate([ya_ref[0, h] for h in range(HEADS)], axis=1)
    mix = _dot(ya, w_ref[0:GROUP_W, :])
    for n, ref in enumerate((yb_ref, yc_ref, yd_ref), start=1):
        mix = mix + _dot(ref[...], w_ref[GROUP_W * n:GROUP_W * (n + 1), :])
    x = _layernorm(alpha * x_ref[...] + mix, g1_ref[...], b1_ref[...])
    xb = x.astype(BF16)
    d_ff = wo_ref.shape[0]
    acc = jnp.zeros(x.shape, F32)
    for c in range(d_ff // FFN_CHUNK):
        lo = FFN_CHUNK * c
        gte = _dot(xb, wi_ref[:, lo:lo + FFN_CHUNK])
        up = _dot(xb, wi_ref[:, d_ff + lo:d_ff + lo + FFN_CHUNK])
        acc = acc + _dot((_silu(gte) * up).astype(BF16), wo_ref[lo:lo + FFN_CHUNK, :])
    o_ref[...] = _layernorm(alpha * x + acc, g2_ref[...], b2_ref[...])


def _mix_ffn(ys, x, w, g1, b1, wi, wo, g2, b2, alpha, tm):
    t, d = x.shape
    assert wo.shape[0] % FFN_CHUNK == 0
    ns = ys[0].shape[2] // tm
    atile = pl.BlockSpec((1, HEADS, tm, HEAD_W), lambda i: (i // ns, 0, i % ns, 0))
    ytile = pl.BlockSpec((tm, GROUP_W), lambda i: (i, 0))
    xtile = pl.BlockSpec((tm, d), lambda i: (i, 0))
    consts = (w, g1, b1, wi, wo, g2, b2)
    return pl.pallas_call(
        functools.partial(_mix_ffn_kernel, alpha=alpha),
        grid=(t // tm,),
        in_specs=[atile] + [ytile] * 3 + [xtile] + [_resident(a.shape) for a in consts],
        out_specs=xtile,
        out_shape=jax.ShapeDtypeStruct((t, d), F32),
        compiler_params=_params(("parallel",)),
        name="mix_ffn",
    )(*ys, x, *consts)


SRC_KR = GROUP_W + MLA_KV_LORA
SRC_ZX = SRC_KR + MLA_ROPE
SRC_DT = SRC_ZX + IN_ZX
SRC_REST = SRC_DT + HEADS
N_IN = SRC_REST + IN_RET + IN_LRU


def _pack_in_proj_kernel(w_ref, o_ref):
    rows = o_ref.shape[0]
    o_ref[:, 0:SRC_KR] = w_ref[0, :, 0:SRC_KR].astype(BF16)
    kr = w_ref[0, :, SRC_KR:SRC_ZX].astype(BF16)
    o_ref[:, SRC_KR:IN_MLA] = jnp.concatenate(
        [kr, jnp.zeros((rows, IN_MLA - SRC_ZX), BF16)], axis=1)
    o_ref[:, IN_MLA:IN_MLA + IN_ZX] = w_ref[0, :, SRC_ZX:SRC_DT].astype(BF16)
    dt0 = IN_MLA + IN_ZX
    for h in range(HEADS):
        col = w_ref[0, :, SRC_DT + h:SRC_DT + h + 1]
        o_ref[:, dt0 + HEAD_W * h:dt0 + HEAD_W * (h + 1)] = jnp.broadcast_to(col, (rows, HEAD_W)).astype(BF16)
    o_ref[:, dt0 + IN_DT:IN_PAD] = w_ref[0, :, SRC_REST:N_IN].astype(BF16)


def _pack_in_proj(w_in, layer):
    _, d, n = w_in.shape
    assert n == N_IN
    return pl.pallas_call(
        _pack_in_proj_kernel,
        grid=(1,),
        in_specs=[pl.BlockSpec((1, d, n), lambda i: (layer, 0, 0))],
        out_specs=pl.BlockSpec((d, IN_PAD), lambda i: (0, 0)),
        out_shape=jax.ShapeDtypeStruct((d, IN_PAD), BF16),
        compiler_params=_params(("arbitrary",)),
        name="pack_in_proj",
    )(w_in)


def _per_head_lanes(w, width):
    k = w.shape[0]
    w = w.reshape(k, HEADS, width)
    return jnp.pad(w, ((0, 0), (0, 0), (0, LANES - width))).reshape(k, HEADS * LANES)


def _row(v):
    return v.reshape(1, -1).astype(F32)


def _head_row(v):
    return jnp.repeat(v.astype(F32), HEAD_W).reshape(1, GROUP_W)


def _block_diag(w):
    nb, n, _ = w.shape
    eye = jnp.eye(nb, dtype=w.dtype)
    return jnp.einsum("gij,gh->gihj", w, eye).reshape(nb * n, nb * n)


def _rope_rows():
    inv_m = ROPE_THETA ** (-jnp.arange(0, MLA_ROPE, 2, dtype=F32) / MLA_ROPE)
    inv_r = ROPE_THETA ** (-jnp.arange(0, HEAD_W, 2, dtype=F32) / HEAD_W)
    zeros = lambda n: jnp.zeros((n,), F32)
    ones = lambda n: jnp.ones((n,), F32)
    half = HEAD_W // 2
    fm = jnp.concatenate([zeros(MLA_NOPE), inv_m, inv_m, zeros(HEAD_W - MLA_QK)])
    sm = jnp.concatenate([zeros(MLA_NOPE), -ones(MLA_ROPE // 2), ones(MLA_ROPE // 2), zeros(HEAD_W - MLA_QK)])
    freq = jnp.concatenate([inv_r, inv_r, fm])
    sign = jnp.concatenate([-ones(half), ones(half), sm])
    return [a.reshape(1, LANES) for a in (freq, sign)]


def _forward(x, positions, p, tiles):
    b, s, d = x.shape
    t = b * s
    depth = p["w_in"].shape[0]
    alpha = (2.0 * depth) ** 0.25
    xf = x.reshape(t, d)
    pos = positions.reshape(t, 1).astype(F32)
    cos_m, sin_m, cos_r, sin_r = _rope_tables(pos, *_rope_rows(), tiles["rope"])

    e_mat = np.zeros((LANES, HEADS * LANES), np.float32)
    vone = np.zeros((1, HEADS * LANES), np.float32)
    for h in range(HEADS):
        for j in range(MLA_ROPE):
            e_mat[j, LANES * h + MLA_NOPE + j] = 1.0
        vone[0, LANES * h + HEAD_W] = 1.0
    e_mat = jnp.asarray(e_mat, BF16)
    vone = jnp.asarray(vone, F32)

    for l in range(depth):
        mla_in, zx, dt, ret_in, lru_in = _in_proj(xf, _pack_in_proj(p["w_in"], l), tiles["proj"])

        w_ukv = p["mla_w_ukv"][l].reshape(MLA_KV_LORA, HEADS, MLA_NOPE + HEAD_W)
        wq = _per_head_lanes(p["mla_w_uq"][l], MLA_QK).astype(BF16)
        wk = _per_head_lanes(w_ukv[:, :, :MLA_NOPE].reshape(MLA_KV_LORA, -1), MLA_NOPE).astype(BF16)
        wv = _per_head_lanes(w_ukv[:, :, MLA_NOPE:].reshape(MLA_KV_LORA, -1), HEAD_W).astype(BF16)
        q, k, v = _mla_prep(mla_in, cos_m, sin_m, _row(p["mla_g_q"][l]), _row(p["mla_g_kv"][l]),
                            wq, wk, wv, e_mat, vone, b, s, tiles["attn"])
        y_a = _attention(q, k, v, tiles["attn"])

        y_b = _ssd(zx, dt, p["ssd_conv_w"][l].astype(F32), _row(p["ssd_conv_b"][l]),
                   _head_row(p["ssd_dt_bias"][l]), _head_row(p["ssd_a_log"][l]), _head_row(p["ssd_d"][l]),
                   _row(p["ssd_norm_g"][l]), b, s, tiles["seq"])
        y_c = _retention(ret_in, cos_r, sin_r, _row(p["ret_gn_g"][l]), _row(p["ret_gn_b"][l]),
                         b, s, tiles["seq"])
        y_d = _rglru(lru_in, p["lru_conv_w"][l].astype(F32), _row(p["lru_conv_b"][l]),
                     _block_diag(p["lru_w_a"][l]).astype(BF16), _row(p["lru_b_a"][l]),
                     _block_diag(p["lru_w_x"][l]).astype(BF16), _row(p["lru_b_x"][l]),
                     _row(p["lru_a_param"][l]), b, s, tiles["seq"])

        xf = _mix_ffn((y_a, y_b, y_c, y_d), xf, p["w_out"][l].astype(BF16), _row(p["ln1_g"][l]),
                      _row(p["ln1_b"][l]), p["w_ffn_in"][l].astype(BF16), p["w_ffn_out"][l].astype(BF16),
                      _row(p["ln2_g"][l]), _row(p["ln2_b"][l]), alpha, tiles["proj"])
    return xf.reshape(b, s, d)


def _tiles(s):
    return {"rope": min(1024, s), "proj": min(512, s), "attn": min(1024, s), "seq": min(256, s)}


def kernel(x, positions, w_in, mla_g_q, mla_w_uq, mla_g_kv, mla_w_ukv, ssd_conv_w, ssd_conv_b, ssd_dt_bias, ssd_a_log, ssd_d, ssd_norm_g, ret_gn_g, ret_gn_b, lru_conv_w, lru_conv_b, lru_w_a, lru_b_a, lru_w_x, lru_b_x, lru_a_param, w_out, ln1_g, ln1_b, w_ffn_in, w_ffn_out, ln2_g, ln2_b):
    p = dict(w_in=w_in, mla_g_q=mla_g_q, mla_w_uq=mla_w_uq, mla_g_kv=mla_g_kv, mla_w_ukv=mla_w_ukv,
             ssd_conv_w=ssd_conv_w, ssd_conv_b=ssd_conv_b, ssd_dt_bias=ssd_dt_bias, ssd_a_log=ssd_a_log,
             ssd_d=ssd_d, ssd_norm_g=ssd_norm_g, ret_gn_g=ret_gn_g, ret_gn_b=ret_gn_b,
             lru_conv_w=lru_conv_w, lru_conv_b=lru_conv_b, lru_w_a=lru_w_a, lru_b_a=lru_b_a,
             lru_w_x=lru_w_x, lru_b_x=lru_b_x, lru_a_param=lru_a_param, w_out=w_out, ln1_g=ln1_g,
             ln1_b=ln1_b, w_ffn_in=w_ffn_in, w_ffn_out=w_ffn_out, ln2_g=ln2_g, ln2_b=ln2_b)
    return _forward(x, positions, p, _tiles(x.shape[1]))
```

```python
import functools
import math

import jax
import jax.numpy as jnp
import numpy as np
from jax import lax
from jax.experimental import pallas as pl
from jax.experimental.pallas import tpu as pltpu

F32 = jnp.float32
BF16 = jnp.bfloat16

GROUP_W = 256
HEADS = 4
HEAD_W = 64
MLA_NOPE = 32
MLA_ROPE = 16
MLA_QK = MLA_NOPE + MLA_ROPE
MLA_KV_LORA = 128
SSD_STATE = 128
SSD_XBC = 768
CHUNK = 128
CONV_K = 4
LRU_C = 8.0
ROPE_THETA = 10000.0
NORM_EPS = 1e-5
HALO = 8
VT_ROWS = 80
ATTN_CHUNK = 256

LANES = 128
VMEM_LIMIT = 56 * 1024 * 1024

IN_MLA = 512
IN_ZX = 1024
IN_DT = 256
IN_RET = 1024
IN_LRU = 512
IN_PAD = IN_MLA + IN_ZX + IN_DT + IN_RET + IN_LRU


def _sigmoid(x):
    return 1.0 / (1.0 + jnp.exp(-x))


def _silu(x):
    return x * _sigmoid(x)


def _softplus(x):
    return jnp.maximum(x, 0.0) + jnp.log1p(jnp.exp(-jnp.abs(x)))


def _gelu_tanh(x):
    c = math.sqrt(2.0 / math.pi)
    return x * (0.5 * (1.0 + jnp.tanh(c * (x + 0.044715 * (x * x * x)))))


def _rms(x, g):
    return x * lax.rsqrt(jnp.mean(x * x, axis=-1, keepdims=True) + NORM_EPS) * g


def _layernorm(v, g, b):
    mu = jnp.mean(v, axis=-1, keepdims=True)
    d = v - mu
    var = jnp.mean(d * d, axis=-1, keepdims=True)
    return d * lax.rsqrt(var + NORM_EPS) * g + b


def _dot(a, b):
    return jnp.dot(a, b, preferred_element_type=F32)


def _dot_nt(a, b):
    return lax.dot_general(a, b, (((1,), (1,)), ((), ())), preferred_element_type=F32)


def _split3(x):
    hi = x.astype(BF16)
    rest = x - hi.astype(F32)
    mid = rest.astype(BF16)
    lo = (rest - mid.astype(F32)).astype(BF16)
    return hi, mid, lo


def _dot_f32_rhs(c, x):
    return sum(_dot(c, part) for part in _split3(x))


def _dot_f32_lhs(x, c):
    return sum(_dot(part, c) for part in _split3(x))


def _resident(shape):
    nd = len(shape)
    return pl.BlockSpec(shape, lambda *_: (0,) * nd, pipeline_mode=pl.Buffered(1))


def _params(sem):
    return pltpu.CompilerParams(dimension_semantics=sem, vmem_limit_bytes=VMEM_LIMIT)


def _rope_tab_kernel(pos_ref, f_ref, s_ref, cm_ref, snm_ref, cr_ref, snr_ref):
    ang = pos_ref[...] * f_ref[...]
    cos = jnp.cos(ang)
    sin = jnp.sin(ang) * s_ref[...]
    cos_sw = pltpu.roll(cos, HEAD_W, 1)
    sin_sw = pltpu.roll(sin, HEAD_W, 1)
    cm_ref[...] = cos_sw
    snm_ref[...] = sin_sw
    lower = lax.broadcasted_iota(jnp.int32, cos.shape, 1) < HEAD_W
    cr_ref[...] = jnp.where(lower, cos, cos_sw)
    snr_ref[...] = jnp.where(lower, sin, sin_sw)


def _rope_tables(pos, freq, sign, tm):
    t = pos.shape[0]
    row = pl.BlockSpec((1, LANES), lambda i: (0, 0))
    tab = pl.BlockSpec((tm, LANES), lambda i: (i, 0))
    return pl.pallas_call(
        _rope_tab_kernel,
        grid=(t // tm,),
        in_specs=[pl.BlockSpec((tm, 1), lambda i: (i, 0)), row, row],
        out_specs=[tab, tab, tab, tab],
        out_shape=[jax.ShapeDtypeStruct((t, LANES), F32)] * 4,
        compiler_params=_params(("parallel",)),
        name="rope_tables",
    )(pos, freq, sign)


def _in_proj_kernel(x_ref, w_ref, mla_ref, zx_ref, dt_ref, ret_ref, lru_ref):
    xb = x_ref[...].astype(BF16)
    start = 0
    for ref in (mla_ref, zx_ref, dt_ref, ret_ref, lru_ref):
        width = ref.shape[-1]
        ref[...] = _dot(xb, w_ref[:, start:start + width]).astype(ref.dtype)
        start += width


def _in_proj(x, w_pad, tm):
    t, d = x.shape
    widths = (IN_MLA, IN_ZX, IN_DT, IN_RET, IN_LRU)
    dtypes = (BF16, BF16, F32, BF16, BF16)
    return pl.pallas_call(
        _in_proj_kernel,
        grid=(t // tm,),
        in_specs=[pl.BlockSpec((tm, d), lambda i: (i, 0)), _resident(w_pad.shape)],
        out_specs=[pl.BlockSpec((tm, w), lambda i: (i, 0)) for w in widths],
        out_shape=[jax.ShapeDtypeStruct((t, w), dt) for w, dt in zip(widths, dtypes)],
        compiler_params=_params(("parallel",)),
        name="in_proj",
    )(x, w_pad)


def _swap_halves(x, lane_in_group, half):
    first = lane_in_group
    return jnp.where(first, pltpu.roll(x, LANES - half, 1), pltpu.roll(x, half, 1))


def _mla_prep_kernel(in_ref, cos_ref, sin_ref, gq_ref, gkv_ref, wq_ref, wq_sw_ref, wk_ref, wv_ref,
                     e_ref, e_sw_ref, vone_ref, qt_out, k_out, vt_out):
    xin = in_ref[...]
    cq = xin[:, :GROUP_W].astype(F32)
    ckv = xin[:, GROUP_W:GROUP_W + MLA_KV_LORA].astype(F32)
    kr = xin[:, GROUP_W + MLA_KV_LORA:]
    nq = _rms(cq, gq_ref[...]).astype(BF16)
    nkv = _rms(ckv, gkv_ref[...]).astype(BF16)
    q = _dot(nq, wq_ref[...])
    k = _dot(nkv, wk_ref[...]) + _dot(kr, e_ref[...])
    v = _dot(nkv, wv_ref[...]) + vone_ref[...]
    q_sw = _dot(nq, wq_sw_ref[...])
    k_sw = _dot(kr, e_sw_ref[...])
    cos = cos_ref[...]
    sin = sin_ref[...]
    scale = MLA_QK ** -0.5 * math.log2(math.e)
    for h in range(HEADS):
        sl = slice(LANES * h, LANES * (h + 1))
        qh = q[:, sl] * cos + q_sw[:, sl] * sin
        kh = k[:, sl] * cos + k_sw[:, sl] * sin
        qt_out[0, h, 0] = (qh * scale).T.astype(BF16)
        k_out[0, h] = kh.astype(BF16)
        vt_out[0, h, 0] = v[:, sl].T[:VT_ROWS, :].astype(BF16)


def _mla_prep(mla_in, cos_m, sin_m, gq, gkv, wq, wq_sw, wk, wv, e_mat, e_sw, vone, b, s, tm):
    ns = s // tm
    consts = (gq, gkv, wq, wq_sw, wk, wv, e_mat, e_sw, vone)
    flat = lambda width: pl.BlockSpec((tm, width), lambda bi, i: (bi * ns + i, 0))
    out = pl.BlockSpec((1, HEADS, tm, LANES), lambda bi, i: (bi, 0, i, 0))
    out_t = lambda rows: pl.BlockSpec((1, HEADS, 1, rows, tm), lambda bi, i: (bi, 0, i, 0, 0))
    t_shape = lambda rows: jax.ShapeDtypeStruct((b, HEADS, ns, rows, tm), BF16)
    return pl.pallas_call(
        _mla_prep_kernel,
        grid=(b, ns),
        in_specs=[flat(IN_MLA), flat(LANES), flat(LANES)] + [_resident(a.shape) for a in consts],
        out_specs=[out_t(LANES), out, out_t(VT_ROWS)],
        out_shape=[t_shape(LANES), jax.ShapeDtypeStruct((b, HEADS, s, LANES), BF16), t_shape(VT_ROWS)],
        compiler_params=_params(("parallel", "parallel")),
        name="mla_prep",
    )(mla_in, cos_m, sin_m, *consts)


def _attn_kernel(qt_ref, k_ref, vt_ref, o_ref, sa_ref, sb_ref, sc_ref, mca_ref, mcb_ref, mcc_ref,
                 m_ref, acc_ref, *, t):
    nt = k_ref.shape[2] // t
    ch = min(ATTN_CHUNK, t)

    def reset():
        m_ref[...] = jnp.full(m_ref.shape, -jnp.inf, F32)
        acc_ref[...] = jnp.zeros(acc_ref.shape, F32)

    def step(fill=None, drain=None):
        if fill is not None:
            fs_ref, fmc_ref, f_qt, f_kt, f_diag = fill
            qt = qt_ref[0, 0, f_qt]
        if drain is not None:
            ds_ref, dmc_ref, d_kt, d_diag = drain
            m_prev = m_ref[0:1, :]
            m_new = jnp.maximum(m_prev, dmc_ref[0:1, :])
            alpha = jnp.exp2(m_prev - m_new)
        for c in range(t // ch):
            cols = slice(ch * c, ch * (c + 1))
            if fill is not None:
                nk = ch * (c + 1) if f_diag else t
                kt = k_ref[0, 0, pl.ds(pl.multiple_of(f_kt * t, t), nk), :]
                st = _dot(kt, qt[:, cols])
                if f_diag:
                    key = lax.broadcasted_iota(jnp.int32, st.shape, 0)
                    qry = lax.broadcasted_iota(jnp.int32, st.shape, 1) + ch * c
                    st = jnp.where(qry >= key, st, -jnp.inf)
                fs_ref[0:nk, cols] = st
                fmc_ref[:, cols] = jnp.broadcast_to(jnp.max(st, axis=0, keepdims=True), (HALO, ch))
            if drain is not None:
                nk = ch * (c + 1) if d_diag else t
                pt = jnp.exp2(ds_ref[0:nk, cols] - m_new[:, cols]).astype(BF16)
                pv = _dot(vt_ref[0, 0, d_kt, :, 0:nk], pt)
                acc_ref[:, cols] = alpha[:, cols] * acc_ref[:, cols] + pv
        if drain is not None:
            m_ref[...] = jnp.broadcast_to(m_new, m_ref.shape)

    def finish(qi):
        acc = acc_ref[...]
        out_t = acc[:HEAD_W, :] / acc[HEAD_W:HEAD_W + 1, :]
        o_ref[0, 0, pl.ds(pl.multiple_of(qi * t, t), t), :] = out_t.T.astype(o_ref.dtype)
        reset()

    reset()
    step(fill=(sc_ref, mcc_ref, 0, 0, True))
    step(drain=(sc_ref, mcc_ref, 0, True))
    finish(0)
    if nt > 1:
        step(fill=(sc_ref, mcc_ref, 1, 1, True))

    def query_tile(qi, carry):
        nxt = jnp.minimum(qi + 1, nt - 1)
        a, b, c = (sa_ref, mca_ref), (sb_ref, mcb_ref), (sc_ref, mcc_ref)
        step(fill=a + (qi, 0, False), drain=c + (qi, True))

        def pair(i, c2):
            step(fill=b + (qi, 2 * i + 1, False), drain=a + (2 * i, False))
            step(fill=a + (qi, 2 * i + 2, False), drain=b + (2 * i + 1, False))
            return c2

        lax.fori_loop(0, (qi - 1) // 2, pair, 0)

        @pl.when(qi % 2 == 1)
        def _():
            step(fill=c + (nxt, nxt, True), drain=a + (qi - 1, False))
            finish(qi)

        @pl.when(qi % 2 == 0)
        def _():
            step(fill=b + (qi, qi - 1, False), drain=a + (qi - 2, False))
            step(fill=c + (nxt, nxt, True), drain=b + (qi - 1, False))
            finish(qi)

        return carry

    lax.fori_loop(1, nt, query_tile, 0)


def _attention(qt, k, vt, t):
    b, heads, s, _ = k.shape
    nt = s // t
    assert qt.shape == (b, heads, nt, LANES, t) and vt.shape == (b, heads, nt, VT_ROWS, t)
    tiles = lambda rows: pl.BlockSpec((1, 1, nt, rows, t), lambda bi, h: (bi, h, 0, 0, 0),
                                      pipeline_mode=pl.Buffered(1))
    return pl.pallas_call(
        functools.partial(_attn_kernel, t=t),
        grid=(b, heads),
        in_specs=[tiles(LANES),
                  pl.BlockSpec((1, 1, s, LANES), lambda bi, h: (bi, h, 0, 0), pipeline_mode=pl.Buffered(1)),
                  tiles(VT_ROWS)],
        out_specs=pl.BlockSpec((1, 1, s, HEAD_W), lambda bi, h: (bi, h, 0, 0)),
        out_shape=jax.ShapeDtypeStruct((b, heads, s, HEAD_W), BF16),
        scratch_shapes=[pltpu.VMEM((t, t + LANES), F32)] * 3 + [pltpu.VMEM((HALO, t), F32)] * 4
                       + [pltpu.VMEM((VT_ROWS, t), F32)],
        compiler_params=_params(("parallel", "parallel")),
        name="mla_attention",
    )(qt, k, vt)


def _causal_conv(x, xe_ref, w, bias):
    ts = x.shape[0]
    xe_ref[HALO:, :] = x
    y = w[CONV_K - 1:CONV_K, :] * x + bias
    for back in range(1, CONV_K):
        tap = w[CONV_K - 1 - back:CONV_K - back, :]
        y = y + tap * xe_ref[pl.ds(HALO - back, ts), :]
    xe_ref[0:HALO, :] = x[ts - HALO:, :]
    return y


def _ssd_kernel(zx_ref, dt_ref, cw_ref, cb_ref, dtb_ref, alog_ref, d_ref, g_ref, o_ref,
                halo_ref, st_ref):
    @pl.when(pl.program_id(1) == 0)
    def _():
        halo_ref[0:HALO, :] = jnp.zeros((HALO, halo_ref.shape[1]), F32)
        st_ref[...] = jnp.zeros(st_ref.shape, F32)

    zx = zx_ref[...]
    ts = zx.shape[0]
    z = zx[:, :GROUP_W].astype(F32)
    xbc = _silu(_causal_conv(zx[:, GROUP_W:].astype(F32), halo_ref, cw_ref[...], cb_ref[...]))
    xs = xbc[:, :GROUP_W]
    bm = xbc[:, GROUP_W:2 * GROUP_W]
    cm = xbc[:, 2 * GROUP_W:]
    dt = _softplus(dt_ref[...] + dtb_ref[...])
    da = dt * (-jnp.exp(alog_ref[...]))
    dtx = dt * xs

    row = lax.broadcasted_iota(jnp.int32, (CHUNK, CHUNK), 0)
    col = lax.broadcasted_iota(jnp.int32, (CHUNK, CHUNK), 1)
    tril = row >= col
    tril_b = tril.astype(BF16)
    lane_head = lax.broadcasted_iota(jnp.int32, (CHUNK, GROUP_W), 1) // HEAD_W

    for c in range(ts // CHUNK):
        r = slice(CHUNK * c, CHUNK * (c + 1))
        acs = _dot_f32_rhs(tril_b, da[r])
        acs_t = acs.T
        a_last = acs[CHUNK - 1:CHUNK, :]
        dtx_c = dtx[r]
        dtx_b = dtx_c.astype(BF16)
        w_end = (jnp.exp(a_last - acs) * dtx_c).astype(BF16)
        y_diag = jnp.zeros((CHUNK, GROUP_W), F32)
        y_off = []
        for g in range(2):
            gs = slice(SSD_STATE * g, SSD_STATE * (g + 1))
            bg = bm[r, gs]
            cg = cm[r, gs].astype(BF16)
            gram = _dot_nt(cg, bg.astype(BF16))
            for h in (2 * g, 2 * g + 1):
                seg = acs[:, HEAD_W * h:HEAD_W * h + 1] - acs_t[HEAD_W * h:HEAD_W * h + 1, :]
                decay = jnp.exp(jnp.where(tril, seg, -jnp.inf))
                yd = _dot((gram * decay).astype(BF16), dtx_b)
                y_diag = jnp.where(lane_head == h, yd, y_diag)
            prev = st_ref[g]
            y_off.append(_dot(cg, prev.astype(BF16)))
            st_ref[g] = prev * jnp.exp(a_last[:, gs]) + _dot(bg.T.astype(BF16), w_end[:, gs])
        y = y_diag + jnp.concatenate(y_off, axis=1) * jnp.exp(acs) + xs[r] * d_ref[...]
        y = y * _silu(z[r])
        o_ref[r, :] = _rms(y, g_ref[...]).astype(o_ref.dtype)


def _ssd(zx, dt, cw, cb, dtb, alog, dskip, g, b, s, ts):
    ns = s // ts
    flat = lambda width: pl.BlockSpec((ts, width), lambda bi, i: (bi * ns + i, 0))
    return pl.pallas_call(
        _ssd_kernel,
        grid=(b, ns),
        in_specs=[flat(IN_ZX), flat(IN_DT)] + [_resident(a.shape) for a in (cw, cb, dtb, alog, dskip, g)],
        out_specs=flat(GROUP_W),
        out_shape=jax.ShapeDtypeStruct((b * s, GROUP_W), BF16),
        scratch_shapes=[pltpu.VMEM((HALO + ts, SSD_XBC), F32), pltpu.VMEM((2, SSD_STATE, 2 * HEAD_W), F32)],
        compiler_params=_params(("parallel", "arbitrary")),
        name="ssd",
    )(zx, dt, cw, cb, dtb, alog, dskip, g)


def _ret_kernel(in_ref, cos_ref, sin_ref, gng_ref, gnb_ref, o_ref, st_ref):
    @pl.when(pl.program_id(1) == 0)
    def _():
        st_ref[...] = jnp.zeros(st_ref.shape, F32)

    xin = in_ref[...]
    ts = xin.shape[0]
    cos = cos_ref[...]
    sin = sin_ref[...]
    lane = lax.broadcasted_iota(jnp.int32, cos.shape, 1)
    first = (lane % HEAD_W) < (HEAD_W // 2)

    def rope(x):
        parts = []
        for half in range(GROUP_W // LANES):
            xh = x[:, LANES * half:LANES * (half + 1)]
            parts.append(xh * cos + _swap_halves(xh, first, HEAD_W // 2) * sin)
        return jnp.concatenate(parts, axis=1)

    q = rope(xin[:, :GROUP_W].astype(F32))
    k = rope(xin[:, GROUP_W:2 * GROUP_W].astype(F32)) * (HEAD_W ** -0.5)
    v = xin[:, 2 * GROUP_W:3 * GROUP_W]
    gate = xin[:, 3 * GROUP_W:].astype(F32)

    log_gamma = [math.log1p(-(2.0 ** (-5.0 - h))) for h in range(HEADS)]
    lane_head = lax.broadcasted_iota(jnp.int32, (CHUNK, GROUP_W), 1) // HEAD_W
    lg_lane = jnp.full((CHUNK, GROUP_W), log_gamma[HEADS - 1], F32)
    for h in range(HEADS - 1):
        lg_lane = jnp.where(lane_head == h, log_gamma[h], lg_lane)
    idx = lax.broadcasted_iota(jnp.int32, (CHUNK, GROUP_W), 0).astype(F32)
    q_dec = jnp.exp(lg_lane * (idx + 1.0))
    k_dec = jnp.exp(lg_lane * (CHUNK - 1.0 - idx))
    c_dec = jnp.exp(lg_lane[:1, :] * float(CHUNK))
    rel = (lax.broadcasted_iota(jnp.int32, (CHUNK, CHUNK), 0)
           - lax.broadcasted_iota(jnp.int32, (CHUNK, CHUNK), 1)).astype(F32)
    blk_r = lax.broadcasted_iota(jnp.int32, (GROUP_W, GROUP_W), 0) // HEAD_W
    blk_c = lax.broadcasted_iota(jnp.int32, (GROUP_W, GROUP_W), 1) // HEAD_W
    same_head = blk_r == blk_c
    avg = jnp.where(same_head, 1.0 / HEAD_W, 0.0).astype(BF16)

    for c in range(ts // CHUNK):
        r = slice(CHUNK * c, CHUNK * (c + 1))
        qc = q[r]
        kc = k[r]
        kb = kc.astype(BF16)
        vb = v[r]
        state = st_ref[...]
        o = _dot(qc.astype(BF16), state.astype(BF16)) * q_dec
        for h in range(HEADS):
            intra = jnp.where(rel >= 0.0, jnp.exp(log_gamma[h] * jnp.maximum(rel, 0.0)), 0.0)
            qh = jnp.where(lane_head == h, qc, 0.0).astype(BF16)
            sc = _dot_nt(qh, kb) * intra
            o = o + jnp.where(lane_head == h, _dot(sc.astype(BF16), vb), 0.0)
        upd = _dot((kc * k_dec).T.astype(BF16), vb)
        st_ref[...] = state * c_dec + jnp.where(same_head, upd, 0.0)
        mu = _dot_f32_lhs(o, avg)
        dlt = o - mu
        var = _dot_f32_lhs(dlt * dlt, avg)
        on = dlt * lax.rsqrt(var + NORM_EPS) * gng_ref[...] + gnb_ref[...]
        o_ref[r, :] = (_silu(gate[r]) * on).astype(o_ref.dtype)


def _retention(ret_in, cos_r, sin_r, gng, gnb, b, s, ts):
    ns = s // ts
    flat = lambda width: pl.BlockSpec((ts, width), lambda bi, i: (bi * ns + i, 0))
    return pl.pallas_call(
        _ret_kernel,
        grid=(b, ns),
        in_specs=[flat(IN_RET), flat(LANES), flat(LANES), _resident(gng.shape), _resident(gnb.shape)],
        out_specs=flat(GROUP_W),
        out_shape=jax.ShapeDtypeStruct((b * s, GROUP_W), BF16),
        scratch_shapes=[pltpu.VMEM((GROUP_W, GROUP_W), F32)],
        compiler_params=_params(("parallel", "arbitrary")),
        name="retention",
    )(ret_in, cos_r, sin_r, gng, gnb)


def _lru_kernel(in_ref, cw_ref, cb_ref, wa_ref, ba_ref, wx_ref, bx_ref, ap_ref, o_ref,
                halo_ref, h_ref):
    @pl.when(pl.program_id(1) == 0)
    def _():
        halo_ref[0:HALO, :] = jnp.zeros((HALO, halo_ref.shape[1]), F32)
        h_ref[...] = jnp.zeros(h_ref.shape, F32)

    xin = in_ref[...]
    ts = xin.shape[0]
    u = _causal_conv(xin[:, :GROUP_W].astype(F32), halo_ref, cw_ref[...], cb_ref[...])
    ub = u.astype(BF16)
    r = _sigmoid(_dot(ub, wa_ref[...]) + ba_ref[...])
    gate_in = _sigmoid(_dot(ub, wx_ref[...]) + bx_ref[...])
    log_a = -LRU_C * r * _softplus(-ap_ref[...])
    a = jnp.exp(log_a)
    b = jnp.sqrt(-jnp.tanh(log_a) * (a * a + 1.0)) * (gate_in * u)

    row = lax.broadcasted_iota(jnp.int32, a.shape, 0)
    stride = 1
    while stride < ts:
        live = row >= stride
        a_sh = jnp.where(live, pltpu.roll(a, stride, 0), 1.0)
        b_sh = jnp.where(live, pltpu.roll(b, stride, 0), 0.0)
        b = a * b_sh + b
        a = a * a_sh
        stride *= 2
    h = a * h_ref[0:1, :] + b
    h_ref[...] = jnp.broadcast_to(h[ts - 1:ts, :], h_ref.shape)
    o_ref[...] = (h * _gelu_tanh(xin[:, GROUP_W:].astype(F32))).astype(o_ref.dtype)


def _rglru(lru_in, cw, cb, wa, ba, wx, bx, ap, b, s, ts):
    ns = s // ts
    flat = lambda width: pl.BlockSpec((ts, width), lambda bi, i: (bi * ns + i, 0))
    return pl.pallas_call(
        _lru_kernel,
        grid=(b, ns),
        in_specs=[flat(IN_LRU)] + [_resident(a.shape) for a in (cw, cb, wa, ba, wx, bx, ap)],
        out_specs=flat(GROUP_W),
        out_shape=jax.ShapeDtypeStruct((b * s, GROUP_W), BF16),
        scratch_shapes=[pltpu.VMEM((HALO + ts, GROUP_W), F32), pltpu.VMEM((HALO, GROUP_W), F32)],
        compiler_params=_params(("parallel", "arbitrary")),
        name="rglru",
    )(lru_in, cw, cb, wa, ba, wx, bx, ap)


FFN_CHUNK = 256


def _mix_ffn_kernel(ya_ref, yb_ref, yc_ref, yd_ref, x_ref, w_ref, g1_ref, b1_ref, wi_ref, wo_ref,
                    g2_ref, b2_ref, o_ref, *, alpha):
    ya = jnp.concatenate([ya_ref[0, h] for h in range(HEADS)], axis=1)
    mix = _dot(ya, w_ref[0:GROUP_W, :])
    for n, ref in enumerate((yb_ref, yc_ref, yd_ref), start=1):
        mix = mix + _dot(ref[...], w_ref[GROUP_W * n:GROUP_W * (n + 1), :])
    x = _layernorm(alpha * x_ref[...] + mix, g1_ref[...], b1_ref[...])
    xb = x.astype(BF16)
    d_ff = wo_ref.shape[0]
    acc = jnp.zeros(x.shape, F32)
    for c in range(d_ff // FFN_CHUNK):
        lo = FFN_CHUNK * c
        gte = _dot(xb, wi_ref[:, lo:lo + FFN_CHUNK])
        up = _dot(xb, wi_ref[:, d_ff + lo:d_ff + lo + FFN_CHUNK])
        acc = acc + _dot((_silu(gte) * up).astype(BF16), wo_ref[lo:lo + FFN_CHUNK, :])
    o_ref[...] = _layernorm(alpha * x + acc, g2_ref[...], b2_ref[...])


def _mix_ffn(ys, x, w, g1, b1, wi, wo, g2, b2, alpha, tm):
    t, d = x.shape
    assert wo.shape[0] % FFN_CHUNK == 0
    ns = ys[0].shape[2] // tm
    atile = pl.BlockSpec((1, HEADS, tm, HEAD_W), lambda i: (i // ns, 0, i % ns, 0))
    ytile = pl.BlockSpec((tm, GROUP_W), lambda i: (i, 0))
    xtile = pl.BlockSpec((tm, d), lambda i: (i, 0))
    consts = (w, g1, b1, wi, wo, g2, b2)
    return pl.pallas_call(
        functools.partial(_mix_ffn_kernel, alpha=alpha),
        grid=(t // tm,),
        in_specs=[atile] + [ytile] * 3 + [xtile] + [_resident(a.shape) for a in consts],
        out_specs=xtile,
        out_shape=jax.ShapeDtypeStruct((t, d), F32),
        compiler_params=_params(("parallel",)),
        name="mix_ffn",
    )(*ys, x, *consts)


SRC_KR = GROUP_W + MLA_KV_LORA
SRC_ZX = SRC_KR + MLA_ROPE
SRC_DT = SRC_ZX + IN_ZX
SRC_REST = SRC_DT + HEADS
N_IN = SRC_REST + IN_RET + IN_LRU


def _pack_in_proj_kernel(w_ref, o_ref):
    rows = o_ref.shape[0]
    o_ref[:, 0:SRC_KR] = w_ref[0, :, 0:SRC_KR].astype(BF16)
    kr = w_ref[0, :, SRC_KR:SRC_ZX].astype(BF16)
    o_ref[:, SRC_KR:IN_MLA] = jnp.concatenate(
        [kr, jnp.zeros((rows, IN_MLA - SRC_ZX), BF16)], axis=1)
    o_ref[:, IN_MLA:IN_MLA + IN_ZX] = w_ref[0, :, SRC_ZX:SRC_DT].astype(BF16)
    dt0 = IN_MLA + IN_ZX
    for h in range(HEADS):
        col = w_ref[0, :, SRC_DT + h:SRC_DT + h + 1]
        o_ref[:, dt0 + HEAD_W * h:dt0 + HEAD_W * (h + 1)] = jnp.broadcast_to(col, (rows, HEAD_W)).astype(BF16)
    o_ref[:, dt0 + IN_DT:IN_PAD] = w_ref[0, :, SRC_REST:N_IN].astype(BF16)


def _pack_in_proj(w_in, layer):
    _, d, n = w_in.shape
    assert n == N_IN
    return pl.pallas_call(
        _pack_in_proj_kernel,
        grid=(1,),
        in_specs=[pl.BlockSpec((1, d, n), lambda i: (layer, 0, 0))],
        out_specs=pl.BlockSpec((d, IN_PAD), lambda i: (0, 0)),
        out_shape=jax.ShapeDtypeStruct((d, IN_PAD), BF16),
        compiler_params=_params(("arbitrary",)),
        name="pack_in_proj",
    )(w_in)


def _per_head_lanes(w, width):
    k = w.shape[0]
    w = w.reshape(k, HEADS, width)
    return jnp.pad(w, ((0, 0), (0, 0), (0, LANES - width))).reshape(k, HEADS * LANES)


def _row(v):
    return v.reshape(1, -1).astype(F32)


def _head_row(v):
    return jnp.repeat(v.astype(F32), HEAD_W).reshape(1, GROUP_W)


def _block_diag(w):
    nb, n, _ = w.shape
    eye = jnp.eye(nb, dtype=w.dtype)
    return jnp.einsum("gij,gh->gihj", w, eye).reshape(nb * n, nb * n)


def _rope_rows():
    inv_m = ROPE_THETA ** (-jnp.arange(0, MLA_ROPE, 2, dtype=F32) / MLA_ROPE)
    inv_r = ROPE_THETA ** (-jnp.arange(0, HEAD_W, 2, dtype=F32) / HEAD_W)
    zeros = lambda n: jnp.zeros((n,), F32)
    ones = lambda n: jnp.ones((n,), F32)
    half = HEAD_W // 2
    fm = jnp.concatenate([zeros(MLA_NOPE), inv_m, inv_m, zeros(HEAD_W - MLA_QK)])
    sm = jnp.concatenate([zeros(MLA_NOPE), -ones(MLA_ROPE // 2), ones(MLA_ROPE // 2), zeros(HEAD_W - MLA_QK)])
    freq = jnp.concatenate([inv_r, inv_r, fm])
    sign = jnp.concatenate([-ones(half), ones(half), sm])
    return [a.reshape(1, LANES) for a in (freq, sign)]


def _forward(x, positions, p, tiles):
    b, s, d = x.shape
    t = b * s
    depth = p["w_in"].shape[0]
    alpha = (2.0 * depth) ** 0.25
    xf = x.reshape(t, d)
    pos = positions.reshape(t, 1).astype(F32)
    cos_m, sin_m, cos_r, sin_r = _rope_tables(pos, *_rope_rows(), tiles["rope"])

    e_mat = np.zeros((LANES, HEADS * LANES), np.float32)
    vone = np.zeros((1, HEADS * LANES), np.float32)
    partner = np.arange(HEADS * LANES)
    in_span = np.zeros((HEADS * LANES,), np.float32)
    half = MLA_ROPE // 2
    for h in range(HEADS):
        lo = LANES * h + MLA_NOPE
        for j in range(MLA_ROPE):
            e_mat[j, lo + j] = 1.0
        partner[lo:lo + half] = np.arange(lo + half, lo + MLA_ROPE)
        partner[lo + half:lo + MLA_ROPE] = np.arange(lo, lo + half)
        in_span[lo:lo + MLA_ROPE] = 1.0
        vone[0, LANES * h + HEAD_W] = 1.0
    e_sw = jnp.asarray(e_mat[:, partner] * in_span, BF16)
    e_mat = jnp.asarray(e_mat, BF16)
    vone = jnp.asarray(vone, F32)

    for l in range(depth):
        mla_in, zx, dt, ret_in, lru_in = _in_proj(xf, _pack_in_proj(p["w_in"], l), tiles["proj"])

        w_ukv = p["mla_w_ukv"][l].reshape(MLA_KV_LORA, HEADS, MLA_NOPE + HEAD_W)
        wq = _per_head_lanes(p["mla_w_uq"][l], MLA_QK).astype(BF16)
        wq_sw = wq[:, partner] * jnp.asarray(in_span, BF16)
        wk = _per_head_lanes(w_ukv[:, :, :MLA_NOPE].reshape(MLA_KV_LORA, -1), MLA_NOPE).astype(BF16)
        wv = _per_head_lanes(w_ukv[:, :, MLA_NOPE:].reshape(MLA_KV_LORA, -1), HEAD_W).astype(BF16)
        q, k, v = _mla_prep(mla_in, cos_m, sin_m, _row(p["mla_g_q"][l]), _row(p["mla_g_kv"][l]),
                            wq, wq_sw, wk, wv, e_mat, e_sw, vone, b, s, tiles["attn"])
        y_a = _attention(q, k, v, tiles["attn"])

        y_b = _ssd(zx, dt, p["ssd_conv_w"][l].astype(F32), _row(p["ssd_conv_b"][l]),
                   _head_row(p["ssd_dt_bias"][l]), _head_row(p["ssd_a_log"][l]), _head_row(p["ssd_d"][l]),
                   _row(p["ssd_norm_g"][l]), b, s, tiles["seq"])
        y_c = _retention(ret_in, cos_r, sin_r, _row(p["ret_gn_g"][l]), _row(p["ret_gn_b"][l]),
                         b, s, tiles["seq"])
        y_d = _rglru(lru_in, p["lru_conv_w"][l].astype(F32), _row(p["lru_conv_b"][l]),
                     _block_diag(p["lru_w_a"][l]).astype(BF16), _row(p["lru_b_a"][l]),
                     _block_diag(p["lru_w_x"][l]).astype(BF16), _row(p["lru_b_x"][l]),
                     _row(p["lru_a_param"][l]), b, s, tiles["seq"])

        xf = _mix_ffn((y_a, y_b, y_c, y_d), xf, p["w_out"][l].astype(BF16), _row(p["ln1_g"][l]),
                      _row(p["ln1_b"][l]), p["w_ffn_in"][l].astype(BF16), p["w_ffn_out"][l].astype(BF16),
                      _row(p["ln2_g"][l]), _row(p["ln2_b"][l]), alpha, tiles["proj"])
    return xf.reshape(b, s, d)


def _tiles(s):
    return {"rope": min(1024, s), "proj": min(512, s), "attn": min(1024, s), "seq": min(256, s)}


def kernel(x, positions, w_in, mla_g_q, mla_w_uq, mla_g_kv, mla_w_ukv, ssd_conv_w, ssd_conv_b, ssd_dt_bias, ssd_a_log, ssd_d, ssd_norm_g, ret_gn_g, ret_gn_b, lru_conv_w, lru_conv_b, lru_w_a, lru_b_a, lru_w_x, lru_b_x, lru_a_param, w_out, ln1_g, ln1_b, w_ffn_in, w_ffn_out, ln2_g, ln2_b):
    p = dict(w_in=w_in, mla_g_q=mla_g_q, mla_w_uq=mla_w_uq, mla_g_kv=mla_g_kv, mla_w_ukv=mla_w_ukv,
             ssd_conv_w=ssd_conv_w, ssd_conv_b=ssd_conv_b, ssd_dt_bias=ssd_dt_bias, ssd_a_log=ssd_a_log,
             ssd_d=ssd_d, ssd_norm_g=ssd_norm_g, ret_gn_g=ret_gn_g, ret_gn_b=ret_gn_b,
             lru_conv_w=lru_conv_w, lru_conv_b=lru_conv_b, lru_w_a=lru_w_a, lru_b_a=lru_b_a,
             lru_w_x=lru_w_x, lru_b_x=lru_b_x, lru_a_param=lru_a_param, w_out=w_out, ln1_g=ln1_g,
             ln1_b=ln1_b, w_ffn_in=w_ffn_in, w_ffn_out=w_ffn_out, ln2_g=ln2_g, ln2_b=ln2_b)
    return _forward(x, positions, p, _tiles(x.shape[1]))
```

```python
import functools
import math

import jax
import jax.numpy as jnp
import numpy as np
from jax import lax
from jax.experimental import pallas as pl
from jax.experimental.pallas import tpu as pltpu

F32 = jnp.float32
BF16 = jnp.bfloat16

GROUP_W = 256
HEADS = 4
HEAD_W = 64
MLA_NOPE = 32
MLA_ROPE = 16
MLA_QK = MLA_NOPE + MLA_ROPE
MLA_KV_LORA = 128
SSD_STATE = 128
SSD_XBC = 768
CHUNK = 128
CONV_K = 4
LRU_C = 8.0
ROPE_THETA = 10000.0
NORM_EPS = 1e-5
HALO = 8
VT_ROWS = 80
ATTN_CHUNK = 256

LANES = 128
VMEM_LIMIT = 56 * 1024 * 1024

IN_MLA = 512
IN_ZX = 1024
IN_DT = 256
IN_RET = 1024
IN_LRU = 512
IN_PAD = IN_MLA + IN_ZX + IN_DT + IN_RET + IN_LRU


def _sigmoid(x):
    return 1.0 / (1.0 + jnp.exp(-x))


def _silu(x):
    return x * _sigmoid(x)


def _softplus(x):
    return jnp.maximum(x, 0.0) + jnp.log1p(jnp.exp(-jnp.abs(x)))


def _gelu_tanh(x):
    c = math.sqrt(2.0 / math.pi)
    return x * (0.5 * (1.0 + jnp.tanh(c * (x + 0.044715 * (x * x * x)))))


def _rms(x, g):
    return x * lax.rsqrt(jnp.mean(x * x, axis=-1, keepdims=True) + NORM_EPS) * g


def _layernorm(v, g, b):
    mu = jnp.mean(v, axis=-1, keepdims=True)
    d = v - mu
    var = jnp.mean(d * d, axis=-1, keepdims=True)
    return d * lax.rsqrt(var + NORM_EPS) * g + b


def _dot(a, b):
    return jnp.dot(a, b, preferred_element_type=F32)


def _dot_nt(a, b):
    return lax.dot_general(a, b, (((1,), (1,)), ((), ())), preferred_element_type=F32)


def _split3(x):
    hi = x.astype(BF16)
    rest = x - hi.astype(F32)
    mid = rest.astype(BF16)
    lo = (rest - mid.astype(F32)).astype(BF16)
    return hi, mid, lo


def _dot_f32_rhs(c, x):
    return sum(_dot(c, part) for part in _split3(x))


def _dot_f32_lhs(x, c):
    return sum(_dot(part, c) for part in _split3(x))


def _resident(shape):
    nd = len(shape)
    return pl.BlockSpec(shape, lambda *_: (0,) * nd, pipeline_mode=pl.Buffered(1))


def _params(sem):
    return pltpu.CompilerParams(dimension_semantics=sem, vmem_limit_bytes=VMEM_LIMIT)


def _rope_tab_kernel(pos_ref, f_ref, s_ref, cm_ref, snm_ref, cr_ref, snr_ref):
    ang = pos_ref[...] * f_ref[...]
    cos = jnp.cos(ang)
    sin = jnp.sin(ang) * s_ref[...]
    cos_sw = pltpu.roll(cos, HEAD_W, 1)
    sin_sw = pltpu.roll(sin, HEAD_W, 1)
    cm_ref[...] = cos_sw
    snm_ref[...] = sin_sw
    lower = lax.broadcasted_iota(jnp.int32, cos.shape, 1) < HEAD_W
    cr_ref[...] = jnp.where(lower, cos, cos_sw)
    snr_ref[...] = jnp.where(lower, sin, sin_sw)


def _rope_tables(pos, freq, sign, tm):
    t = pos.shape[0]
    row = pl.BlockSpec((1, LANES), lambda i: (0, 0))
    tab = pl.BlockSpec((tm, LANES), lambda i: (i, 0))
    return pl.pallas_call(
        _rope_tab_kernel,
        grid=(t // tm,),
        in_specs=[pl.BlockSpec((tm, 1), lambda i: (i, 0)), row, row],
        out_specs=[tab, tab, tab, tab],
        out_shape=[jax.ShapeDtypeStruct((t, LANES), F32)] * 4,
        compiler_params=_params(("parallel",)),
        name="rope_tables",
    )(pos, freq, sign)


def _in_proj_kernel(x_ref, w_ref, mla_ref, zx_ref, dt_ref, ret_ref, lru_ref):
    xb = x_ref[...].astype(BF16)
    start = 0
    for ref in (mla_ref, zx_ref, dt_ref, ret_ref, lru_ref):
        width = ref.shape[-1]
        ref[...] = _dot(xb, w_ref[:, start:start + width]).astype(ref.dtype)
        start += width


def _in_proj(x, w_pad, tm):
    t, d = x.shape
    widths = (IN_MLA, IN_ZX, IN_DT, IN_RET, IN_LRU)
    dtypes = (BF16, BF16, F32, BF16, BF16)
    return pl.pallas_call(
        _in_proj_kernel,
        grid=(t // tm,),
        in_specs=[pl.BlockSpec((tm, d), lambda i: (i, 0)), _resident(w_pad.shape)],
        out_specs=[pl.BlockSpec((tm, w), lambda i: (i, 0)) for w in widths],
        out_shape=[jax.ShapeDtypeStruct((t, w), dt) for w, dt in zip(widths, dtypes)],
        compiler_params=_params(("parallel",)),
        name="in_proj",
    )(x, w_pad)


def _swap_halves(x, lane_in_group, half):
    first = lane_in_group
    return jnp.where(first, pltpu.roll(x, LANES - half, 1), pltpu.roll(x, half, 1))


def _mla_prep_kernel(in_ref, cos_ref, sin_ref, gq_ref, gkv_ref, wq_ref, wq_sw_ref, wk_ref, wv_ref,
                     e_ref, e_sw_ref, vone_ref, qt_out, k_out, vt_out):
    xin = in_ref[...]
    cq = xin[:, :GROUP_W].astype(F32)
    ckv = xin[:, GROUP_W:GROUP_W + MLA_KV_LORA].astype(F32)
    kr = xin[:, GROUP_W + MLA_KV_LORA:]
    nq = _rms(cq, gq_ref[...]).astype(BF16)
    nkv = _rms(ckv, gkv_ref[...]).astype(BF16)
    q = _dot(nq, wq_ref[...])
    k = _dot(nkv, wk_ref[...]) + _dot(kr, e_ref[...])
    v = _dot(nkv, wv_ref[...]) + vone_ref[...]
    q_sw = _dot(nq, wq_sw_ref[...])
    k_sw = _dot(kr, e_sw_ref[...])
    cos = cos_ref[...]
    sin = sin_ref[...]
    scale = MLA_QK ** -0.5 * math.log2(math.e)
    for h in range(HEADS):
        sl = slice(LANES * h, LANES * (h + 1))
        qh = q[:, sl] * cos + q_sw[:, sl] * sin
        kh = k[:, sl] * cos + k_sw[:, sl] * sin
        qt_out[0, h, 0] = (qh * scale).T.astype(BF16)
        k_out[0, h] = kh.astype(BF16)
        vt_out[0, h, 0] = v[:, sl].T[:VT_ROWS, :].astype(BF16)


def _mla_prep(mla_in, cos_m, sin_m, gq, gkv, wq, wq_sw, wk, wv, e_mat, e_sw, vone, b, s, tm):
    ns = s // tm
    consts = (gq, gkv, wq, wq_sw, wk, wv, e_mat, e_sw, vone)
    flat = lambda width: pl.BlockSpec((tm, width), lambda bi, i: (bi * ns + i, 0))
    out = pl.BlockSpec((1, HEADS, tm, LANES), lambda bi, i: (bi, 0, i, 0))
    out_t = lambda rows: pl.BlockSpec((1, HEADS, 1, rows, tm), lambda bi, i: (bi, 0, i, 0, 0))
    t_shape = lambda rows: jax.ShapeDtypeStruct((b, HEADS, ns, rows, tm), BF16)
    return pl.pallas_call(
        _mla_prep_kernel,
        grid=(b, ns),
        in_specs=[flat(IN_MLA), flat(LANES), flat(LANES)] + [_resident(a.shape) for a in consts],
        out_specs=[out_t(LANES), out, out_t(VT_ROWS)],
        out_shape=[t_shape(LANES), jax.ShapeDtypeStruct((b, HEADS, s, LANES), BF16), t_shape(VT_ROWS)],
        compiler_params=_params(("parallel", "parallel")),
        name="mla_prep",
    )(mla_in, cos_m, sin_m, *consts)


def _attn_kernel(qt_ref, k_ref, vt_ref, o_ref, sa_ref, sb_ref, sc_ref, mca_ref, mcb_ref, mcc_ref,
                 m_ref, acc_ref, *, t):
    nt = k_ref.shape[2] // t
    ch = min(ATTN_CHUNK, t)

    def reset():
        m_ref[...] = jnp.full(m_ref.shape, -jnp.inf, F32)
        acc_ref[...] = jnp.zeros(acc_ref.shape, F32)

    def step(fill=None, drain=None):
        if fill is not None:
            fs_ref, fmc_ref, f_qt, f_kt, f_diag = fill
            qt = qt_ref[0, 0, f_qt]
        if drain is not None:
            ds_ref, dmc_ref, d_kt, d_diag = drain
            m_prev = m_ref[0:1, :]
            m_new = jnp.maximum(m_prev, dmc_ref[0:1, :])
            alpha = jnp.exp2(m_prev - m_new)
        for c in range(t // ch):
            cols = slice(ch * c, ch * (c + 1))
            if fill is not None:
                nk = ch * (c + 1) if f_diag else t
                kt = k_ref[0, 0, pl.ds(pl.multiple_of(f_kt * t, t), nk), :]
                st = _dot(kt, qt[:, cols])
                if f_diag:
                    key = lax.broadcasted_iota(jnp.int32, st.shape, 0)
                    qry = lax.broadcasted_iota(jnp.int32, st.shape, 1) + ch * c
                    st = jnp.where(qry >= key, st, -jnp.inf)
                fs_ref[0:nk, cols] = st
                fmc_ref[:, cols] = jnp.broadcast_to(jnp.max(st, axis=0, keepdims=True), (HALO, ch))
            if drain is not None:
                nk = ch * (c + 1) if d_diag else t
                pt = jnp.exp2(ds_ref[0:nk, cols] - m_new[:, cols]).astype(BF16)
                pv = _dot(vt_ref[0, 0, d_kt, :, 0:nk], pt)
                acc_ref[:, cols] = alpha[:, cols] * acc_ref[:, cols] + pv
        if drain is not None:
            m_ref[...] = jnp.broadcast_to(m_new, m_ref.shape)

    def finish(qi):
        acc = acc_ref[...]
        out_t = acc[:HEAD_W, :] / acc[HEAD_W:HEAD_W + 1, :]
        o_ref[0, 0, pl.ds(pl.multiple_of(qi * t, t), t), :] = out_t.T.astype(o_ref.dtype)
        reset()

    reset()
    step(fill=(sc_ref, mcc_ref, 0, 0, True))
    step(drain=(sc_ref, mcc_ref, 0, True))
    finish(0)
    if nt > 1:
        step(fill=(sc_ref, mcc_ref, 1, 1, True))

    def query_tile(qi, carry):
        nxt = jnp.minimum(qi + 1, nt - 1)
        a, b, c = (sa_ref, mca_ref), (sb_ref, mcb_ref), (sc_ref, mcc_ref)
        step(fill=a + (qi, 0, False), drain=c + (qi, True))

        def pair(i, c2):
            step(fill=b + (qi, 2 * i + 1, False), drain=a + (2 * i, False))
            step(fill=a + (qi, 2 * i + 2, False), drain=b + (2 * i + 1, False))
            return c2

        lax.fori_loop(0, (qi - 1) // 2, pair, 0)

        @pl.when(qi % 2 == 1)
        def _():
            step(fill=c + (nxt, nxt, True), drain=a + (qi - 1, False))
            finish(qi)

        @pl.when(qi % 2 == 0)
        def _():
            step(fill=b + (qi, qi - 1, False), drain=a + (qi - 2, False))
            step(fill=c + (nxt, nxt, True), drain=b + (qi - 1, False))
            finish(qi)

        return carry

    lax.fori_loop(1, nt, query_tile, 0)


def _attention(qt, k, vt, t):
    b, heads, s, _ = k.shape
    nt = s // t
    assert qt.shape == (b, heads, nt, LANES, t) and vt.shape == (b, heads, nt, VT_ROWS, t)
    tiles = lambda rows: pl.BlockSpec((1, 1, nt, rows, t), lambda bi, h: (bi, h, 0, 0, 0))
    return pl.pallas_call(
        functools.partial(_attn_kernel, t=t),
        grid=(b, heads),
        in_specs=[tiles(LANES), pl.BlockSpec((1, 1, s, LANES), lambda bi, h: (bi, h, 0, 0)), tiles(VT_ROWS)],
        out_specs=pl.BlockSpec((1, 1, s, HEAD_W), lambda bi, h: (bi, h, 0, 0)),
        out_shape=jax.ShapeDtypeStruct((b, heads, s, HEAD_W), BF16),
        scratch_shapes=[pltpu.VMEM((t, t + LANES), F32)] * 3 + [pltpu.VMEM((HALO, t), F32)] * 4
                       + [pltpu.VMEM((VT_ROWS, t), F32)],
        compiler_params=_params(("parallel", "parallel")),
        name="mla_attention",
    )(qt, k, vt)


def _causal_conv(x, xe_ref, w, bias):
    ts = x.shape[0]
    xe_ref[HALO:, :] = x
    y = w[CONV_K - 1:CONV_K, :] * x + bias
    for back in range(1, CONV_K):
        tap = w[CONV_K - 1 - back:CONV_K - back, :]
        y = y + tap * xe_ref[pl.ds(HALO - back, ts), :]
    xe_ref[0:HALO, :] = x[ts - HALO:, :]
    return y


def _ssd_kernel(zx_ref, dt_ref, cw_ref, cb_ref, dtb_ref, alog_ref, d_ref, g_ref, o_ref,
                halo_ref, st_ref):
    @pl.when(pl.program_id(1) == 0)
    def _():
        halo_ref[0:HALO, :] = jnp.zeros((HALO, halo_ref.shape[1]), F32)
        st_ref[...] = jnp.zeros(st_ref.shape, F32)

    zx = zx_ref[...]
    ts = zx.shape[0]
    z = zx[:, :GROUP_W].astype(F32)
    xbc = _silu(_causal_conv(zx[:, GROUP_W:].astype(F32), halo_ref, cw_ref[...], cb_ref[...]))
    xs = xbc[:, :GROUP_W]
    bm = xbc[:, GROUP_W:2 * GROUP_W]
    cm = xbc[:, 2 * GROUP_W:]
    dt = _softplus(dt_ref[...] + dtb_ref[...])
    da = dt * (-jnp.exp(alog_ref[...]))
    dtx = dt * xs

    row = lax.broadcasted_iota(jnp.int32, (CHUNK, CHUNK), 0)
    col = lax.broadcasted_iota(jnp.int32, (CHUNK, CHUNK), 1)
    tril = row >= col
    tril_b = tril.astype(BF16)
    lane_head = lax.broadcasted_iota(jnp.int32, (CHUNK, GROUP_W), 1) // HEAD_W

    for c in range(ts // CHUNK):
        r = slice(CHUNK * c, CHUNK * (c + 1))
        acs = _dot_f32_rhs(tril_b, da[r])
        acs_t = acs.T
        a_last = acs[CHUNK - 1:CHUNK, :]
        dtx_c = dtx[r]
        dtx_b = dtx_c.astype(BF16)
        w_end = (jnp.exp(a_last - acs) * dtx_c).astype(BF16)
        y_diag = jnp.zeros((CHUNK, GROUP_W), F32)
        y_off = []
        for g in range(2):
            gs = slice(SSD_STATE * g, SSD_STATE * (g + 1))
            bg = bm[r, gs]
            cg = cm[r, gs].astype(BF16)
            gram = _dot_nt(cg, bg.astype(BF16))
            for h in (2 * g, 2 * g + 1):
                seg = acs[:, HEAD_W * h:HEAD_W * h + 1] - acs_t[HEAD_W * h:HEAD_W * h + 1, :]
                decay = jnp.exp(jnp.where(tril, seg, -jnp.inf))
                yd = _dot((gram * decay).astype(BF16), dtx_b)
                y_diag = jnp.where(lane_head == h, yd, y_diag)
            prev = st_ref[g]
            y_off.append(_dot(cg, prev.astype(BF16)))
            st_ref[g] = prev * jnp.exp(a_last[:, gs]) + _dot(bg.T.astype(BF16), w_end[:, gs])
        y = y_diag + jnp.concatenate(y_off, axis=1) * jnp.exp(acs) + xs[r] * d_ref[...]
        y = y * _silu(z[r])
        o_ref[r, :] = _rms(y, g_ref[...]).astype(o_ref.dtype)


def _ssd(zx, dt, cw, cb, dtb, alog, dskip, g, b, s, ts):
    ns = s // ts
    flat = lambda width: pl.BlockSpec((ts, width), lambda bi, i: (bi * ns + i, 0))
    return pl.pallas_call(
        _ssd_kernel,
        grid=(b, ns),
        in_specs=[flat(IN_ZX), flat(IN_DT)] + [_resident(a.shape) for a in (cw, cb, dtb, alog, dskip, g)],
        out_specs=flat(GROUP_W),
        out_shape=jax.ShapeDtypeStruct((b * s, GROUP_W), BF16),
        scratch_shapes=[pltpu.VMEM((HALO + ts, SSD_XBC), F32), pltpu.VMEM((2, SSD_STATE, 2 * HEAD_W), F32)],
        compiler_params=_params(("parallel", "arbitrary")),
        name="ssd",
    )(zx, dt, cw, cb, dtb, alog, dskip, g)


def _ret_kernel(in_ref, cos_ref, sin_ref, gng_ref, gnb_ref, o_ref, st_ref):
    @pl.when(pl.program_id(1) == 0)
    def _():
        st_ref[...] = jnp.zeros(st_ref.shape, F32)

    xin = in_ref[...]
    ts = xin.shape[0]
    cos = cos_ref[...]
    sin = sin_ref[...]
    lane = lax.broadcasted_iota(jnp.int32, cos.shape, 1)
    first = (lane % HEAD_W) < (HEAD_W // 2)

    def rope(x):
        parts = []
        for half in range(GROUP_W // LANES):
            xh = x[:, LANES * half:LANES * (half + 1)]
            parts.append(xh * cos + _swap_halves(xh, first, HEAD_W // 2) * sin)
        return jnp.concatenate(parts, axis=1)

    q = rope(xin[:, :GROUP_W].astype(F32))
    k = rope(xin[:, GROUP_W:2 * GROUP_W].astype(F32)) * (HEAD_W ** -0.5)
    v = xin[:, 2 * GROUP_W:3 * GROUP_W]
    gate = xin[:, 3 * GROUP_W:].astype(F32)

    log_gamma = [math.log1p(-(2.0 ** (-5.0 - h))) for h in range(HEADS)]
    lane_head = lax.broadcasted_iota(jnp.int32, (CHUNK, GROUP_W), 1) // HEAD_W
    lg_lane = jnp.full((CHUNK, GROUP_W), log_gamma[HEADS - 1], F32)
    for h in range(HEADS - 1):
        lg_lane = jnp.where(lane_head == h, log_gamma[h], lg_lane)
    idx = lax.broadcasted_iota(jnp.int32, (CHUNK, GROUP_W), 0).astype(F32)
    q_dec = jnp.exp(lg_lane * (idx + 1.0))
    k_dec = jnp.exp(lg_lane * (CHUNK - 1.0 - idx))
    c_dec = jnp.exp(lg_lane[:1, :] * float(CHUNK))
    rel = (lax.broadcasted_iota(jnp.int32, (CHUNK, CHUNK), 0)
           - lax.broadcasted_iota(jnp.int32, (CHUNK, CHUNK), 1)).astype(F32)
    blk_r = lax.broadcasted_iota(jnp.int32, (GROUP_W, GROUP_W), 0) // HEAD_W
    blk_c = lax.broadcasted_iota(jnp.int32, (GROUP_W, GROUP_W), 1) // HEAD_W
    same_head = blk_r == blk_c
    avg = jnp.where(same_head, 1.0 / HEAD_W, 0.0).astype(BF16)

    for c in range(ts // CHUNK):
        r = slice(CHUNK * c, CHUNK * (c + 1))
        qc = q[r]
        kc = k[r]
        kb = kc.astype(BF16)
        vb = v[r]
        state = st_ref[...]
        o = _dot(qc.astype(BF16), state.astype(BF16)) * q_dec
        for h in range(HEADS):
            intra = jnp.where(rel >= 0.0, jnp.exp(log_gamma[h] * jnp.maximum(rel, 0.0)), 0.0)
            qh = jnp.where(lane_head == h, qc, 0.0).astype(BF16)
            sc = _dot_nt(qh, kb) * intra
            o = o + jnp.where(lane_head == h, _dot(sc.astype(BF16), vb), 0.0)
        upd = _dot((kc * k_dec).T.astype(BF16), vb)
        st_ref[...] = state * c_dec + jnp.where(same_head, upd, 0.0)
        mu = _dot_f32_lhs(o, avg)
        dlt = o - mu
        var = _dot_f32_lhs(dlt * dlt, avg)
        on = dlt * lax.rsqrt(var + NORM_EPS) * gng_ref[...] + gnb_ref[...]
        o_ref[r, :] = (_silu(gate[r]) * on).astype(o_ref.dtype)


def _retention(ret_in, cos_r, sin_r, gng, gnb, b, s, ts):
    ns = s // ts
    flat = lambda width: pl.BlockSpec((ts, width), lambda bi, i: (bi * ns + i, 0))
    return pl.pallas_call(
        _ret_kernel,
        grid=(b, ns),
        in_specs=[flat(IN_RET), flat(LANES), flat(LANES), _resident(gng.shape), _resident(gnb.shape)],
        out_specs=flat(GROUP_W),
        out_shape=jax.ShapeDtypeStruct((b * s, GROUP_W), BF16),
        scratch_shapes=[pltpu.VMEM((GROUP_W, GROUP_W), F32)],
        compiler_params=_params(("parallel", "arbitrary")),
        name="retention",
    )(ret_in, cos_r, sin_r, gng, gnb)


def _lru_kernel(in_ref, cw_ref, cb_ref, wa_ref, ba_ref, wx_ref, bx_ref, ap_ref, o_ref,
                halo_ref, h_ref):
    @pl.when(pl.program_id(1) == 0)
    def _():
        halo_ref[0:HALO, :] = jnp.zeros((HALO, halo_ref.shape[1]), F32)
        h_ref[...] = jnp.zeros(h_ref.shape, F32)

    xin = in_ref[...]
    ts = xin.shape[0]
    u = _causal_conv(xin[:, :GROUP_W].astype(F32), halo_ref, cw_ref[...], cb_ref[...])
    ub = u.astype(BF16)
    r = _sigmoid(_dot(ub, wa_ref[...]) + ba_ref[...])
    gate_in = _sigmoid(_dot(ub, wx_ref[...]) + bx_ref[...])
    log_a = -LRU_C * r * _softplus(-ap_ref[...])
    a = jnp.exp(log_a)
    b = jnp.sqrt(-jnp.tanh(log_a) * (a * a + 1.0)) * (gate_in * u)

    row = lax.broadcasted_iota(jnp.int32, a.shape, 0) % HALO
    stride = 1
    while stride < HALO:
        live = row >= stride
        a_sh = jnp.where(live, pltpu.roll(a, stride, 0), 1.0)
        b_sh = jnp.where(live, pltpu.roll(b, stride, 0), 0.0)
        b = a * b_sh + b
        a = a * a_sh
        stride *= 2
    carry = h_ref[0:1, :]
    tiles_h = []
    for g in range(ts // HALO):
        rows = slice(HALO * g, HALO * (g + 1))
        hg = a[rows] * carry + b[rows]
        tiles_h.append(hg)
        carry = hg[HALO - 1:HALO, :]
    h = jnp.concatenate(tiles_h, axis=0)
    h_ref[...] = jnp.broadcast_to(h[ts - 1:ts, :], h_ref.shape)
    o_ref[...] = (h * _gelu_tanh(xin[:, GROUP_W:].astype(F32))).astype(o_ref.dtype)


def _rglru(lru_in, cw, cb, wa, ba, wx, bx, ap, b, s, ts):
    ns = s // ts
    flat = lambda width: pl.BlockSpec((ts, width), lambda bi, i: (bi * ns + i, 0))
    return pl.pallas_call(
        _lru_kernel,
        grid=(b, ns),
        in_specs=[flat(IN_LRU)] + [_resident(a.shape) for a in (cw, cb, wa, ba, wx, bx, ap)],
        out_specs=flat(GROUP_W),
        out_shape=jax.ShapeDtypeStruct((b * s, GROUP_W), BF16),
        scratch_shapes=[pltpu.VMEM((HALO + ts, GROUP_W), F32), pltpu.VMEM((HALO, GROUP_W), F32)],
        compiler_params=_params(("parallel", "arbitrary")),
        name="rglru",
    )(lru_in, cw, cb, wa, ba, wx, bx, ap)


FFN_CHUNK = 256


def _mix_ffn_kernel(ya_ref, yb_ref, yc_ref, yd_ref, x_ref, w_ref, g1_ref, b1_ref, wi_ref, wo_ref,
                    g2_ref, b2_ref, o_ref, *, alpha):
    ya = jnp.concatenate([ya_ref[0, h] for h in range(HEADS)], axis=1)
    mix = _dot(ya, w_ref[0:GROUP_W, :])
    for n, ref in enumerate((yb_ref, yc_ref, yd_ref), start=1):
        mix = mix + _dot(ref[...], w_ref[GROUP_W * n:GROUP_W * (n + 1), :])
    x = _layernorm(alpha * x_ref[...] + mix, g1_ref[...], b1_ref[...])
    xb = x.astype(BF16)
    d_ff = wo_ref.shape[0]
    acc = jnp.zeros(x.shape, F32)
    for c in range(d_ff // FFN_CHUNK):
        lo = FFN_CHUNK * c
        gte = _dot(xb, wi_ref[:, lo:lo + FFN_CHUNK])
        up = _dot(xb, wi_ref[:, d_ff + lo:d_ff + lo + FFN_CHUNK])
        acc = acc + _dot((_silu(gte) * up).astype(BF16), wo_ref[lo:lo + FFN_CHUNK, :])
    o_ref[...] = _layernorm(alpha * x + acc, g2_ref[...], b2_ref[...])


def _mix_ffn(ys, x, w, g1, b1, wi, wo, g2, b2, alpha, tm):
    t, d = x.shape
    assert wo.shape[0] % FFN_CHUNK == 0
    ns = ys[0].shape[2] // tm
    atile = pl.BlockSpec((1, HEADS, tm, HEAD_W), lambda i: (i // ns, 0, i % ns, 0))
    ytile = pl.BlockSpec((tm, GROUP_W), lambda i: (i, 0))
    xtile = pl.BlockSpec((tm, d), lambda i: (i, 0))
    consts = (w, g1, b1, wi, wo, g2, b2)
    return pl.pallas_call(
        functools.partial(_mix_ffn_kernel, alpha=alpha),
        grid=(t // tm,),
        in_specs=[atile] + [ytile] * 3 + [xtile] + [_resident(a.shape) for a in consts],
        out_specs=xtile,
        out_shape=jax.ShapeDtypeStruct((t, d), F32),
        compiler_params=_params(("parallel",)),
        name="mix_ffn",
    )(*ys, x, *consts)


SRC_KR = GROUP_W + MLA_KV_LORA
SRC_ZX = SRC_KR + MLA_ROPE
SRC_DT = SRC_ZX + IN_ZX
SRC_REST = SRC_DT + HEADS
N_IN = SRC_REST + IN_RET + IN_LRU


def _pack_in_proj_kernel(w_ref, o_ref):
    rows = o_ref.shape[0]
    o_ref[:, 0:SRC_KR] = w_ref[0, :, 0:SRC_KR].astype(BF16)
    kr = w_ref[0, :, SRC_KR:SRC_ZX].astype(BF16)
    o_ref[:, SRC_KR:IN_MLA] = jnp.concatenate(
        [kr, jnp.zeros((rows, IN_MLA - SRC_ZX), BF16)], axis=1)
    o_ref[:, IN_MLA:IN_MLA + IN_ZX] = w_ref[0, :, SRC_ZX:SRC_DT].astype(BF16)
    dt0 = IN_MLA + IN_ZX
    for h in range(HEADS):
        col = w_ref[0, :, SRC_DT + h:SRC_DT + h + 1]
        o_ref[:, dt0 + HEAD_W * h:dt0 + HEAD_W * (h + 1)] = jnp.broadcast_to(col, (rows, HEAD_W)).astype(BF16)
    o_ref[:, dt0 + IN_DT:IN_PAD] = w_ref[0, :, SRC_REST:N_IN].astype(BF16)


def _pack_in_proj(w_in, layer):
    _, d, n = w_in.shape
    assert n == N_IN
    return pl.pallas_call(
        _pack_in_proj_kernel,
        grid=(1,),
        in_specs=[pl.BlockSpec((1, d, n), lambda i: (layer, 0, 0))],
        out_specs=pl.BlockSpec((d, IN_PAD), lambda i: (0, 0)),
        out_shape=jax.ShapeDtypeStruct((d, IN_PAD), BF16),
        compiler_params=_params(("arbitrary",)),
        name="pack_in_proj",
    )(w_in)


def _per_head_lanes(w, width):
    k = w.shape[0]
    w = w.reshape(k, HEADS, width)
    return jnp.pad(w, ((0, 0), (0, 0), (0, LANES - width))).reshape(k, HEADS * LANES)


def _row(v):
    return v.reshape(1, -1).astype(F32)


def _head_row(v):
    return jnp.repeat(v.astype(F32), HEAD_W).reshape(1, GROUP_W)


def _block_diag(w):
    nb, n, _ = w.shape
    eye = jnp.eye(nb, dtype=w.dtype)
    return jnp.einsum("gij,gh->gihj", w, eye).reshape(nb * n, nb * n)


def _rope_rows():
    inv_m = ROPE_THETA ** (-jnp.arange(0, MLA_ROPE, 2, dtype=F32) / MLA_ROPE)
    inv_r = ROPE_THETA ** (-jnp.arange(0, HEAD_W, 2, dtype=F32) / HEAD_W)
    zeros = lambda n: jnp.zeros((n,), F32)
    ones = lambda n: jnp.ones((n,), F32)
    half = HEAD_W // 2
    fm = jnp.concatenate([zeros(MLA_NOPE), inv_m, inv_m, zeros(HEAD_W - MLA_QK)])
    sm = jnp.concatenate([zeros(MLA_NOPE), -ones(MLA_ROPE // 2), ones(MLA_ROPE // 2), zeros(HEAD_W - MLA_QK)])
    freq = jnp.concatenate([inv_r, inv_r, fm])
    sign = jnp.concatenate([-ones(half), ones(half), sm])
    return [a.reshape(1, LANES) for a in (freq, sign)]


def _forward(x, positions, p, tiles):
    b, s, d = x.shape
    t = b * s
    depth = p["w_in"].shape[0]
    alpha = (2.0 * depth) ** 0.25
    xf = x.reshape(t, d)
    pos = positions.reshape(t, 1).astype(F32)
    cos_m, sin_m, cos_r, sin_r = _rope_tables(pos, *_rope_rows(), tiles["rope"])

    e_mat = np.zeros((LANES, HEADS * LANES), np.float32)
    vone = np.zeros((1, HEADS * LANES), np.float32)
    partner = np.arange(HEADS * LANES)
    in_span = np.zeros((HEADS * LANES,), np.float32)
    half = MLA_ROPE // 2
    for h in range(HEADS):
        lo = LANES * h + MLA_NOPE
        for j in range(MLA_ROPE):
            e_mat[j, lo + j] = 1.0
        partner[lo:lo + half] = np.arange(lo + half, lo + MLA_ROPE)
        partner[lo + half:lo + MLA_ROPE] = np.arange(lo, lo + half)
        in_span[lo:lo + MLA_ROPE] = 1.0
        vone[0, LANES * h + HEAD_W] = 1.0
    e_sw = jnp.asarray(e_mat[:, partner] * in_span, BF16)
    e_mat = jnp.asarray(e_mat, BF16)
    vone = jnp.asarray(vone, F32)

    for l in range(depth):
        mla_in, zx, dt, ret_in, lru_in = _in_proj(xf, _pack_in_proj(p["w_in"], l), tiles["proj"])

        w_ukv = p["mla_w_ukv"][l].reshape(MLA_KV_LORA, HEADS, MLA_NOPE + HEAD_W)
        wq = _per_head_lanes(p["mla_w_uq"][l], MLA_QK).astype(BF16)
        wq_sw = wq[:, partner] * jnp.asarray(in_span, BF16)
        wk = _per_head_lanes(w_ukv[:, :, :MLA_NOPE].reshape(MLA_KV_LORA, -1), MLA_NOPE).astype(BF16)
        wv = _per_head_lanes(w_ukv[:, :, MLA_NOPE:].reshape(MLA_KV_LORA, -1), HEAD_W).astype(BF16)
        q, k, v = _mla_prep(mla_in, cos_m, sin_m, _row(p["mla_g_q"][l]), _row(p["mla_g_kv"][l]),
                            wq, wq_sw, wk, wv, e_mat, e_sw, vone, b, s, tiles["attn"])
        y_a = _attention(q, k, v, tiles["attn"])

        y_b = _ssd(zx, dt, p["ssd_conv_w"][l].astype(F32), _row(p["ssd_conv_b"][l]),
                   _head_row(p["ssd_dt_bias"][l]), _head_row(p["ssd_a_log"][l]), _head_row(p["ssd_d"][l]),
                   _row(p["ssd_norm_g"][l]), b, s, tiles["ssd"])
        y_c = _retention(ret_in, cos_r, sin_r, _row(p["ret_gn_g"][l]), _row(p["ret_gn_b"][l]),
                         b, s, tiles["ret"])
        y_d = _rglru(lru_in, p["lru_conv_w"][l].astype(F32), _row(p["lru_conv_b"][l]),
                     _block_diag(p["lru_w_a"][l]).astype(BF16), _row(p["lru_b_a"][l]),
                     _block_diag(p["lru_w_x"][l]).astype(BF16), _row(p["lru_b_x"][l]),
                     _row(p["lru_a_param"][l]), b, s, tiles["lru"])

        xf = _mix_ffn((y_a, y_b, y_c, y_d), xf, p["w_out"][l].astype(BF16), _row(p["ln1_g"][l]),
                      _row(p["ln1_b"][l]), p["w_ffn_in"][l].astype(BF16), p["w_ffn_out"][l].astype(BF16),
                      _row(p["ln2_g"][l]), _row(p["ln2_b"][l]), alpha, tiles["proj"])
    return xf.reshape(b, s, d)


def _tiles(s):
    return {"rope": min(1024, s), "proj": min(512, s), "attn": min(1024, s),
            "ssd": min(1024, s), "ret": min(1024, s), "lru": min(1024, s)}


def kernel(x, positions, w_in, mla_g_q, mla_w_uq, mla_g_kv, mla_w_ukv, ssd_conv_w, ssd_conv_b, ssd_dt_bias, ssd_a_log, ssd_d, ssd_norm_g, ret_gn_g, ret_gn_b, lru_conv_w, lru_conv_b, lru_w_a, lru_b_a, lru_w_x, lru_b_x, lru_a_param, w_out, ln1_g, ln1_b, w_ffn_in, w_ffn_out, ln2_g, ln2_b):
    p = dict(w_in=w_in, mla_g_q=mla_g_q, mla_w_uq=mla_w_uq, mla_g_kv=mla_g_kv, mla_w_ukv=mla_w_ukv,
             ssd_conv_w=ssd_conv_w, ssd_conv_b=ssd_conv_b, ssd_dt_bias=ssd_dt_bias, ssd_a_log=ssd_a_log,
             ssd_d=ssd_d, ssd_norm_g=ssd_norm_g, ret_gn_g=ret_gn_g, ret_gn_b=ret_gn_b,
             lru_conv_w=lru_conv_w, lru_conv_b=lru_conv_b, lru_w_a=lru_w_a, lru_b_a=lru_b_a,
             lru_w_x=lru_w_x, lru_b_x=lru_b_x, lru_a_param=lru_a_param, w_out=w_out, ln1_g=ln1_g,
             ln1_b=ln1_b, w_ffn_in=w_ffn_in, w_ffn_out=w_ffn_out, ln2_g=ln2_g, ln2_b=ln2_b)
    return _forward(x, positions, p, _tiles(x.shape[1]))
```

```python
import functools
import math

import jax
import jax.numpy as jnp
import numpy as np
from jax import lax
from jax.experimental import pallas as pl
from jax.experimental.pallas import tpu as pltpu

F32 = jnp.float32
BF16 = jnp.bfloat16

GROUP_W = 256
HEADS = 4
HEAD_W = 64
MLA_NOPE = 32
MLA_ROPE = 16
MLA_QK = MLA_NOPE + MLA_ROPE
MLA_KV_LORA = 128
SSD_STATE = 128
SSD_XBC = 768
CHUNK = 128
CONV_K = 4
LRU_C = 8.0
ROPE_THETA = 10000.0
NORM_EPS = 1e-5
HALO = 8
VT_ROWS = 80
ATTN_CHUNK = 256

LANES = 128
VMEM_LIMIT = 56 * 1024 * 1024

IN_MLA = 512
IN_ZX = 1024
IN_DT = 256
IN_RET = 1024
IN_LRU = 512
IN_PAD = IN_MLA + IN_ZX + IN_DT + IN_RET + IN_LRU


def _sigmoid(x):
    return 1.0 / (1.0 + jnp.exp(-x))


def _silu(x):
    return x * _sigmoid(x)


def _softplus(x):
    return jnp.maximum(x, 0.0) + jnp.log1p(jnp.exp(-jnp.abs(x)))


def _gelu_tanh(x):
    c = math.sqrt(2.0 / math.pi)
    return x * (0.5 * (1.0 + jnp.tanh(c * (x + 0.044715 * (x * x * x)))))


def _rms(x, g):
    return x * lax.rsqrt(jnp.mean(x * x, axis=-1, keepdims=True) + NORM_EPS) * g


def _layernorm(v, g, b):
    mu = jnp.mean(v, axis=-1, keepdims=True)
    d = v - mu
    var = jnp.mean(d * d, axis=-1, keepdims=True)
    return d * lax.rsqrt(var + NORM_EPS) * g + b


def _dot(a, b):
    return jnp.dot(a, b, preferred_element_type=F32)


def _dot_nt(a, b):
    return lax.dot_general(a, b, (((1,), (1,)), ((), ())), preferred_element_type=F32)


def _split3(x):
    hi = x.astype(BF16)
    rest = x - hi.astype(F32)
    mid = rest.astype(BF16)
    lo = (rest - mid.astype(F32)).astype(BF16)
    return hi, mid, lo


def _dot_f32_rhs(c, x):
    return sum(_dot(c, part) for part in _split3(x))


def _dot_f32_lhs(x, c):
    return sum(_dot(part, c) for part in _split3(x))


def _resident(shape):
    nd = len(shape)
    return pl.BlockSpec(shape, lambda *_: (0,) * nd, pipeline_mode=pl.Buffered(1))


def _params(sem):
    return pltpu.CompilerParams(dimension_semantics=sem, vmem_limit_bytes=VMEM_LIMIT)


def _rope_tab_kernel(pos_ref, f_ref, s_ref, cm_ref, snm_ref, cr_ref, snr_ref):
    ang = pos_ref[...] * f_ref[...]
    cos = jnp.cos(ang)
    sin = jnp.sin(ang) * s_ref[...]
    cos_sw = pltpu.roll(cos, HEAD_W, 1)
    sin_sw = pltpu.roll(sin, HEAD_W, 1)
    cm_ref[...] = cos_sw
    snm_ref[...] = sin_sw
    lower = lax.broadcasted_iota(jnp.int32, cos.shape, 1) < HEAD_W
    cr_ref[...] = jnp.where(lower, cos, cos_sw)
    snr_ref[...] = jnp.where(lower, sin, sin_sw)


def _rope_tables(pos, freq, sign, tm):
    t = pos.shape[0]
    row = pl.BlockSpec((1, LANES), lambda i: (0, 0))
    tab = pl.BlockSpec((tm, LANES), lambda i: (i, 0))
    return pl.pallas_call(
        _rope_tab_kernel,
        grid=(t // tm,),
        in_specs=[pl.BlockSpec((tm, 1), lambda i: (i, 0)), row, row],
        out_specs=[tab, tab, tab, tab],
        out_shape=[jax.ShapeDtypeStruct((t, LANES), F32)] * 4,
        compiler_params=_params(("parallel",)),
        name="rope_tables",
    )(pos, freq, sign)


def _in_proj_kernel(x_ref, w_ref, mla_ref, zx_ref, dt_ref, ret_ref, lru_ref):
    xb = x_ref[...].astype(BF16)
    start = 0
    for ref in (mla_ref, zx_ref, dt_ref, ret_ref, lru_ref):
        width = ref.shape[-1]
        ref[...] = _dot(xb, w_ref[:, start:start + width]).astype(ref.dtype)
        start += width


def _in_proj(x, w_pad, tm):
    t, d = x.shape
    widths = (IN_MLA, IN_ZX, IN_DT, IN_RET, IN_LRU)
    dtypes = (BF16, BF16, F32, BF16, BF16)
    return pl.pallas_call(
        _in_proj_kernel,
        grid=(t // tm,),
        in_specs=[pl.BlockSpec((tm, d), lambda i: (i, 0)), _resident(w_pad.shape)],
        out_specs=[pl.BlockSpec((tm, w), lambda i: (i, 0)) for w in widths],
        out_shape=[jax.ShapeDtypeStruct((t, w), dt) for w, dt in zip(widths, dtypes)],
        compiler_params=_params(("parallel",)),
        name="in_proj",
    )(x, w_pad)


def _swap_halves(x, lane_in_group, half):
    first = lane_in_group
    return jnp.where(first, pltpu.roll(x, LANES - half, 1), pltpu.roll(x, half, 1))


def _mla_prep_kernel(in_ref, cos_ref, sin_ref, gq_ref, gkv_ref, wq_ref, wq_sw_ref, wk_ref, wv_ref,
                     e_ref, e_sw_ref, vone_ref, qt_out, k_out, vt_out):
    xin = in_ref[...]
    cq = xin[:, :GROUP_W].astype(F32)
    ckv = xin[:, GROUP_W:GROUP_W + MLA_KV_LORA].astype(F32)
    kr = xin[:, GROUP_W + MLA_KV_LORA:]
    nq = _rms(cq, gq_ref[...]).astype(BF16)
    nkv = _rms(ckv, gkv_ref[...]).astype(BF16)
    q = _dot(nq, wq_ref[...])
    k = _dot(nkv, wk_ref[...]) + _dot(kr, e_ref[...])
    v = _dot(nkv, wv_ref[...]) + vone_ref[...]
    q_sw = _dot(nq, wq_sw_ref[...])
    k_sw = _dot(kr, e_sw_ref[...])
    cos = cos_ref[...]
    sin = sin_ref[...]
    scale = MLA_QK ** -0.5 * math.log2(math.e)
    for h in range(HEADS):
        sl = slice(LANES * h, LANES * (h + 1))
        qh = q[:, sl] * cos + q_sw[:, sl] * sin
        kh = k[:, sl] * cos + k_sw[:, sl] * sin
        qt_out[0, h, 0] = (qh * scale).T.astype(BF16)
        k_out[0, h] = kh.astype(BF16)
        vt_out[0, h, 0] = v[:, sl].T[:VT_ROWS, :].astype(BF16)


def _mla_prep(mla_in, cos_m, sin_m, gq, gkv, wq, wq_sw, wk, wv, e_mat, e_sw, vone, b, s, tm):
    ns = s // tm
    consts = (gq, gkv, wq, wq_sw, wk, wv, e_mat, e_sw, vone)
    flat = lambda width: pl.BlockSpec((tm, width), lambda bi, i: (bi * ns + i, 0))
    out = pl.BlockSpec((1, HEADS, tm, LANES), lambda bi, i: (bi, 0, i, 0))
    out_t = lambda rows: pl.BlockSpec((1, HEADS, 1, rows, tm), lambda bi, i: (bi, 0, i, 0, 0))
    t_shape = lambda rows: jax.ShapeDtypeStruct((b, HEADS, ns, rows, tm), BF16)
    return pl.pallas_call(
        _mla_prep_kernel,
        grid=(b, ns),
        in_specs=[flat(IN_MLA), flat(LANES), flat(LANES)] + [_resident(a.shape) for a in consts],
        out_specs=[out_t(LANES), out, out_t(VT_ROWS)],
        out_shape=[t_shape(LANES), jax.ShapeDtypeStruct((b, HEADS, s, LANES), BF16), t_shape(VT_ROWS)],
        compiler_params=_params(("parallel", "parallel")),
        name="mla_prep",
    )(mla_in, cos_m, sin_m, *consts)


def _attn_kernel(qt_ref, k_ref, vt_ref, o_ref, sa_ref, sb_ref, sc_ref, mca_ref, mcb_ref, mcc_ref,
                 m_ref, acc_ref, *, t):
    nt = k_ref.shape[2] // t
    ch = min(ATTN_CHUNK, t)

    def reset():
        m_ref[...] = jnp.full(m_ref.shape, -jnp.inf, F32)
        acc_ref[...] = jnp.zeros(acc_ref.shape, F32)

    def step(fill=None, drain=None):
        if fill is not None:
            fs_ref, fmc_ref, f_qt, f_kt, f_diag = fill
            qt = qt_ref[0, 0, f_qt]
        if drain is not None:
            ds_ref, dmc_ref, d_kt, d_diag = drain
            m_prev = m_ref[0:1, :]
            m_new = jnp.maximum(m_prev, dmc_ref[0:1, :])
            alpha = jnp.exp2(m_prev - m_new)
        for c in range(t // ch):
            cols = slice(ch * c, ch * (c + 1))
            if fill is not None:
                nk = ch * (c + 1) if f_diag else t
                kt = k_ref[0, 0, pl.ds(pl.multiple_of(f_kt * t, t), nk), :]
                st = _dot(kt, qt[:, cols])
                if f_diag:
                    key = lax.broadcasted_iota(jnp.int32, st.shape, 0)
                    qry = lax.broadcasted_iota(jnp.int32, st.shape, 1) + ch * c
                    st = jnp.where(qry >= key, st, -jnp.inf)
                fs_ref[0:nk, cols] = st
                fmc_ref[:, cols] = jnp.broadcast_to(jnp.max(st, axis=0, keepdims=True), (HALO, ch))
            if drain is not None:
                nk = ch * (c + 1) if d_diag else t
                pt = jnp.exp2(ds_ref[0:nk, cols] - m_new[:, cols]).astype(BF16)
                pv = _dot(vt_ref[0, 0, d_kt, :, 0:nk], pt)
                acc_ref[:, cols] = alpha[:, cols] * acc_ref[:, cols] + pv
        if drain is not None:
            m_ref[...] = jnp.broadcast_to(m_new, m_ref.shape)

    def finish(qi):
        acc = acc_ref[...]
        out_t = acc[:HEAD_W, :] / acc[HEAD_W:HEAD_W + 1, :]
        o_ref[0, 0, pl.ds(pl.multiple_of(qi * t, t), t), :] = out_t.T.astype(o_ref.dtype)
        reset()

    reset()
    step(fill=(sc_ref, mcc_ref, 0, 0, True))
    step(drain=(sc_ref, mcc_ref, 0, True))
    finish(0)
    if nt > 1:
        step(fill=(sc_ref, mcc_ref, 1, 1, True))

    def query_tile(qi, carry):
        nxt = jnp.minimum(qi + 1, nt - 1)
        a, b, c = (sa_ref, mca_ref), (sb_ref, mcb_ref), (sc_ref, mcc_ref)
        step(fill=a + (qi, 0, False), drain=c + (qi, True))

        def pair(i, c2):
            step(fill=b + (qi, 2 * i + 1, False), drain=a + (2 * i, False))
            step(fill=a + (qi, 2 * i + 2, False), drain=b + (2 * i + 1, False))
            return c2

        lax.fori_loop(0, (qi - 1) // 2, pair, 0)

        @pl.when(qi % 2 == 1)
        def _():
            step(fill=c + (nxt, nxt, True), drain=a + (qi - 1, False))
            finish(qi)

        @pl.when(qi % 2 == 0)
        def _():
            step(fill=b + (qi, qi - 1, False), drain=a + (qi - 2, False))
            step(fill=c + (nxt, nxt, True), drain=b + (qi - 1, False))
            finish(qi)

        return carry

    lax.fori_loop(1, nt, query_tile, 0)


def _attention(qt, k, vt, t):
    b, heads, s, _ = k.shape
    nt = s // t
    assert qt.shape == (b, heads, nt, LANES, t) and vt.shape == (b, heads, nt, VT_ROWS, t)
    tiles = lambda rows: pl.BlockSpec((1, 1, nt, rows, t), lambda bi, h: (bi, h, 0, 0, 0))
    return pl.pallas_call(
        functools.partial(_attn_kernel, t=t),
        grid=(b, heads),
        in_specs=[tiles(LANES), pl.BlockSpec((1, 1, s, LANES), lambda bi, h: (bi, h, 0, 0)), tiles(VT_ROWS)],
        out_specs=pl.BlockSpec((1, 1, s, HEAD_W), lambda bi, h: (bi, h, 0, 0)),
        out_shape=jax.ShapeDtypeStruct((b, heads, s, HEAD_W), BF16),
        scratch_shapes=[pltpu.VMEM((t, t + LANES), F32)] * 3 + [pltpu.VMEM((HALO, t), F32)] * 4
                       + [pltpu.VMEM((VT_ROWS, t), F32)],
        compiler_params=_params(("parallel", "parallel")),
        name="mla_attention",
    )(qt, k, vt)


def _causal_conv(x, xe_ref, w, bias):
    ts = x.shape[0]
    xe_ref[HALO:, :] = x
    y = w[CONV_K - 1:CONV_K, :] * x + bias
    for back in range(1, CONV_K):
        tap = w[CONV_K - 1 - back:CONV_K - back, :]
        y = y + tap * xe_ref[pl.ds(HALO - back, ts), :]
    xe_ref[0:HALO, :] = x[ts - HALO:, :]
    return y


def _ssd_kernel(zx_ref, dt_ref, cw_ref, cb_ref, dtb_ref, alog_ref, d_ref, g_ref, o_ref,
                halo_ref, st_ref):
    @pl.when(pl.program_id(1) == 0)
    def _():
        halo_ref[0:HALO, :] = jnp.zeros((HALO, halo_ref.shape[1]), F32)
        st_ref[...] = jnp.zeros(st_ref.shape, F32)

    zx = zx_ref[...]
    ts = zx.shape[0]
    z = zx[:, :GROUP_W].astype(F32)
    xbc = _silu(_causal_conv(zx[:, GROUP_W:].astype(F32), halo_ref, cw_ref[...], cb_ref[...]))
    xs = xbc[:, :GROUP_W]
    bm = xbc[:, GROUP_W:2 * GROUP_W]
    cm = xbc[:, 2 * GROUP_W:]
    dt = _softplus(dt_ref[...] + dtb_ref[...])
    da = dt * (-jnp.exp(alog_ref[...]))
    dtx = dt * xs

    row = lax.broadcasted_iota(jnp.int32, (CHUNK, CHUNK), 0)
    col = lax.broadcasted_iota(jnp.int32, (CHUNK, CHUNK), 1)
    tril = row >= col
    tril_b = tril.astype(BF16)
    lane_head = lax.broadcasted_iota(jnp.int32, (CHUNK, GROUP_W), 1) // HEAD_W

    for c in range(ts // CHUNK):
        r = slice(CHUNK * c, CHUNK * (c + 1))
        acs = _dot_f32_rhs(tril_b, da[r])
        acs_t = acs.T
        a_last = acs[CHUNK - 1:CHUNK, :]
        dtx_c = dtx[r]
        dtx_b = dtx_c.astype(BF16)
        w_end = (jnp.exp(a_last - acs) * dtx_c).astype(BF16)
        y_diag = jnp.zeros((CHUNK, GROUP_W), F32)
        y_off = []
        for g in range(2):
            gs = slice(SSD_STATE * g, SSD_STATE * (g + 1))
            bg = bm[r, gs]
            cg = cm[r, gs].astype(BF16)
            gram = _dot_nt(cg, bg.astype(BF16))
            for h in (2 * g, 2 * g + 1):
                seg = acs[:, HEAD_W * h:HEAD_W * h + 1] - acs_t[HEAD_W * h:HEAD_W * h + 1, :]
                decay = jnp.exp(jnp.where(tril, seg, -jnp.inf))
                yd = _dot((gram * decay).astype(BF16), dtx_b)
                y_diag = jnp.where(lane_head == h, yd, y_diag)
            prev = st_ref[g]
            y_off.append(_dot(cg, prev.astype(BF16)))
            st_ref[g] = prev * jnp.exp(a_last[:, gs]) + _dot(bg.T.astype(BF16), w_end[:, gs])
        y = y_diag + jnp.concatenate(y_off, axis=1) * jnp.exp(acs) + xs[r] * d_ref[...]
        y = y * _silu(z[r])
        o_ref[r, :] = _rms(y, g_ref[...]).astype(o_ref.dtype)


def _ssd(zx, dt, cw, cb, dtb, alog, dskip, g, b, s, ts):
    ns = s // ts
    flat = lambda width: pl.BlockSpec((ts, width), lambda bi, i: (bi * ns + i, 0))
    return pl.pallas_call(
        _ssd_kernel,
        grid=(b, ns),
        in_specs=[flat(IN_ZX), flat(IN_DT)] + [_resident(a.shape) for a in (cw, cb, dtb, alog, dskip, g)],
        out_specs=flat(GROUP_W),
        out_shape=jax.ShapeDtypeStruct((b * s, GROUP_W), BF16),
        scratch_shapes=[pltpu.VMEM((HALO + ts, SSD_XBC), F32), pltpu.VMEM((2, SSD_STATE, 2 * HEAD_W), F32)],
        compiler_params=_params(("parallel", "arbitrary")),
        name="ssd",
    )(zx, dt, cw, cb, dtb, alog, dskip, g)


def _ret_kernel(in_ref, cos_ref, sin_ref, gng_ref, gnb_ref, o_ref, st_ref):
    @pl.when(pl.program_id(1) == 0)
    def _():
        st_ref[...] = jnp.zeros(st_ref.shape, F32)

    xin = in_ref[...]
    ts = xin.shape[0]
    cos = cos_ref[...]
    sin = sin_ref[...]
    lane = lax.broadcasted_iota(jnp.int32, cos.shape, 1)
    first = (lane % HEAD_W) < (HEAD_W // 2)

    def rope(x):
        parts = []
        for half in range(GROUP_W // LANES):
            xh = x[:, LANES * half:LANES * (half + 1)]
            parts.append(xh * cos + _swap_halves(xh, first, HEAD_W // 2) * sin)
        return jnp.concatenate(parts, axis=1)

    q = rope(xin[:, :GROUP_W].astype(F32))
    k = rope(xin[:, GROUP_W:2 * GROUP_W].astype(F32)) * (HEAD_W ** -0.5)
    v = xin[:, 2 * GROUP_W:3 * GROUP_W]
    gate = xin[:, 3 * GROUP_W:].astype(F32)

    log_gamma = [math.log1p(-(2.0 ** (-5.0 - h))) for h in range(HEADS)]
    lane_head = lax.broadcasted_iota(jnp.int32, (CHUNK, GROUP_W), 1) // HEAD_W
    lg_lane = jnp.full((CHUNK, GROUP_W), log_gamma[HEADS - 1], F32)
    for h in range(HEADS - 1):
        lg_lane = jnp.where(lane_head == h, log_gamma[h], lg_lane)
    idx = lax.broadcasted_iota(jnp.int32, (CHUNK, GROUP_W), 0).astype(F32)
    q_dec = jnp.exp(lg_lane * (idx + 1.0))
    k_dec = jnp.exp(lg_lane * (CHUNK - 1.0 - idx))
    c_dec = jnp.exp(lg_lane[:1, :] * float(CHUNK))
    rel = (lax.broadcasted_iota(jnp.int32, (CHUNK, CHUNK), 0)
           - lax.broadcasted_iota(jnp.int32, (CHUNK, CHUNK), 1)).astype(F32)
    blk_r = lax.broadcasted_iota(jnp.int32, (GROUP_W, GROUP_W), 0) // HEAD_W
    blk_c = lax.broadcasted_iota(jnp.int32, (GROUP_W, GROUP_W), 1) // HEAD_W
    same_head = blk_r == blk_c
    avg = jnp.where(same_head, 1.0 / HEAD_W, 0.0).astype(BF16)

    for c in range(ts // CHUNK):
        r = slice(CHUNK * c, CHUNK * (c + 1))
        qc = q[r]
        kc = k[r]
        kb = kc.astype(BF16)
        vb = v[r]
        state = st_ref[...]
        o = _dot(qc.astype(BF16), state.astype(BF16)) * q_dec
        for h in range(HEADS):
            intra = jnp.where(rel >= 0.0, jnp.exp(log_gamma[h] * jnp.maximum(rel, 0.0)), 0.0)
            qh = jnp.where(lane_head == h, qc, 0.0).astype(BF16)
            sc = _dot_nt(qh, kb) * intra
            o = o + jnp.where(lane_head == h, _dot(sc.astype(BF16), vb), 0.0)
        upd = _dot((kc * k_dec).T.astype(BF16), vb)
        st_ref[...] = state * c_dec + jnp.where(same_head, upd, 0.0)
        mu = _dot_f32_lhs(o, avg)
        dlt = o - mu
        var = _dot_f32_lhs(dlt * dlt, avg)
        on = dlt * lax.rsqrt(var + NORM_EPS) * gng_ref[...] + gnb_ref[...]
        o_ref[r, :] = (_silu(gate[r]) * on).astype(o_ref.dtype)


def _retention(ret_in, cos_r, sin_r, gng, gnb, b, s, ts):
    ns = s // ts
    flat = lambda width: pl.BlockSpec((ts, width), lambda bi, i: (bi * ns + i, 0))
    return pl.pallas_call(
        _ret_kernel,
        grid=(b, ns),
        in_specs=[flat(IN_RET), flat(LANES), flat(LANES), _resident(gng.shape), _resident(gnb.shape)],
        out_specs=flat(GROUP_W),
        out_shape=jax.ShapeDtypeStruct((b * s, GROUP_W), BF16),
        scratch_shapes=[pltpu.VMEM((GROUP_W, GROUP_W), F32)],
        compiler_params=_params(("parallel", "arbitrary")),
        name="retention",
    )(ret_in, cos_r, sin_r, gng, gnb)


def _lru_kernel(in_ref, cw_ref, cb_ref, wa_ref, ba_ref, wx_ref, bx_ref, ap_ref, o_ref,
                halo_ref, h_ref):
    @pl.when(pl.program_id(1) == 0)
    def _():
        halo_ref[0:HALO, :] = jnp.zeros((HALO, halo_ref.shape[1]), F32)
        h_ref[...] = jnp.zeros(h_ref.shape, F32)

    xin = in_ref[...]
    ts = xin.shape[0]
    u = _causal_conv(xin[:, :GROUP_W].astype(F32), halo_ref, cw_ref[...], cb_ref[...])
    ub = u.astype(BF16)
    r = _sigmoid(_dot(ub, wa_ref[...]) + ba_ref[...])
    gate_in = _sigmoid(_dot(ub, wx_ref[...]) + bx_ref[...])
    log_a = -LRU_C * r * _softplus(-ap_ref[...])
    a = jnp.exp(log_a)
    b = jnp.sqrt(-jnp.tanh(log_a) * (a * a + 1.0)) * (gate_in * u)

    row = lax.broadcasted_iota(jnp.int32, a.shape, 0) % HALO
    stride = 1
    while stride < HALO:
        live = row >= stride
        a_sh = jnp.where(live, pltpu.roll(a, stride, 0), 1.0)
        b_sh = jnp.where(live, pltpu.roll(b, stride, 0), 0.0)
        b = a * b_sh + b
        a = a * a_sh
        stride *= 2
    carry = h_ref[0:1, :]
    tiles_h = []
    for g in range(ts // HALO):
        rows = slice(HALO * g, HALO * (g + 1))
        hg = a[rows] * carry + b[rows]
        tiles_h.append(hg)
        carry = hg[HALO - 1:HALO, :]
    h = jnp.concatenate(tiles_h, axis=0)
    h_ref[...] = jnp.broadcast_to(h[ts - 1:ts, :], h_ref.shape)
    o_ref[...] = (h * _gelu_tanh(xin[:, GROUP_W:].astype(F32))).astype(o_ref.dtype)


def _rglru(lru_in, cw, cb, wa, ba, wx, bx, ap, b, s, ts):
    ns = s // ts
    flat = lambda width: pl.BlockSpec((ts, width), lambda bi, i: (bi * ns + i, 0))
    return pl.pallas_call(
        _lru_kernel,
        grid=(b, ns),
        in_specs=[flat(IN_LRU)] + [_resident(a.shape) for a in (cw, cb, wa, ba, wx, bx, ap)],
        out_specs=flat(GROUP_W),
        out_shape=jax.ShapeDtypeStruct((b * s, GROUP_W), BF16),
        scratch_shapes=[pltpu.VMEM((HALO + ts, GROUP_W), F32), pltpu.VMEM((HALO, GROUP_W), F32)],
        compiler_params=_params(("parallel", "arbitrary")),
        name="rglru",
    )(lru_in, cw, cb, wa, ba, wx, bx, ap)


FFN_CHUNK = 256


def _mix_ffn_kernel(ya_ref, yb_ref, yc_ref, yd_ref, x_ref, w_ref, g1_ref, b1_ref, wi_ref, wo_ref,
                    g2_ref, b2_ref, o_ref, *, alpha):
    ya = jnp.concatenate([ya_ref[0, h] for h in range(HEADS)], axis=1)
    mix = _dot(ya, w_ref[0:GROUP_W, :])
    for n, ref in enumerate((yb_ref, yc_ref, yd_ref), start=1):
        mix = mix + _dot(ref[...], w_ref[GROUP_W * n:GROUP_W * (n + 1), :])
    x = _layernorm(alpha * x_ref[...] + mix, g1_ref[...], b1_ref[...])
    xb = x.astype(BF16)
    d_ff = wo_ref.shape[0]
    acc = jnp.zeros(x.shape, F32)
    for c in range(d_ff // FFN_CHUNK):
        lo = FFN_CHUNK * c
        gte = _dot(xb, wi_ref[:, lo:lo + FFN_CHUNK])
        up = _dot(xb, wi_ref[:, d_ff + lo:d_ff + lo + FFN_CHUNK])
        acc = acc + _dot((_silu(gte) * up).astype(BF16), wo_ref[lo:lo + FFN_CHUNK, :])
    o_ref[...] = _layernorm(alpha * x + acc, g2_ref[...], b2_ref[...])


def _mix_ffn(ys, x, w, g1, b1, wi, wo, g2, b2, alpha, tm):
    t, d = x.shape
    assert wo.shape[0] % FFN_CHUNK == 0
    ns = ys[0].shape[2] // tm
    atile = pl.BlockSpec((1, HEADS, tm, HEAD_W), lambda i: (i // ns, 0, i % ns, 0))
    ytile = pl.BlockSpec((tm, GROUP_W), lambda i: (i, 0))
    xtile = pl.BlockSpec((tm, d), lambda i: (i, 0))
    consts = (w, g1, b1, wi, wo, g2, b2)
    return pl.pallas_call(
        functools.partial(_mix_ffn_kernel, alpha=alpha),
        grid=(t // tm,),
        in_specs=[atile] + [ytile] * 3 + [xtile] + [_resident(a.shape) for a in consts],
        out_specs=xtile,
        out_shape=jax.ShapeDtypeStruct((t, d), F32),
        compiler_params=_params(("parallel",)),
        name="mix_ffn",
    )(*ys, x, *consts)


SRC_KR = GROUP_W + MLA_KV_LORA
SRC_ZX = SRC_KR + MLA_ROPE
SRC_DT = SRC_ZX + IN_ZX
SRC_REST = SRC_DT + HEADS
N_IN = SRC_REST + IN_RET + IN_LRU


def _pack_in_proj_kernel(w_ref, o_ref):
    rows = o_ref.shape[0]
    o_ref[:, 0:SRC_KR] = w_ref[0, :, 0:SRC_KR].astype(BF16)
    kr = w_ref[0, :, SRC_KR:SRC_ZX].astype(BF16)
    o_ref[:, SRC_KR:IN_MLA] = jnp.concatenate(
        [kr, jnp.zeros((rows, IN_MLA - SRC_ZX), BF16)], axis=1)
    o_ref[:, IN_MLA:IN_MLA + IN_ZX] = w_ref[0, :, SRC_ZX:SRC_DT].astype(BF16)
    dt0 = IN_MLA + IN_ZX
    for h in range(HEADS):
        col = w_ref[0, :, SRC_DT + h:SRC_DT + h + 1]
        o_ref[:, dt0 + HEAD_W * h:dt0 + HEAD_W * (h + 1)] = jnp.broadcast_to(col, (rows, HEAD_W)).astype(BF16)
    o_ref[:, dt0 + IN_DT:IN_PAD] = w_ref[0, :, SRC_REST:N_IN].astype(BF16)


def _pack_in_proj(w_in, layer):
    _, d, n = w_in.shape
    assert n == N_IN
    return pl.pallas_call(
        _pack_in_proj_kernel,
        grid=(1,),
        in_specs=[pl.BlockSpec((1, d, n), lambda i: (layer, 0, 0))],
        out_specs=pl.BlockSpec((d, IN_PAD), lambda i: (0, 0)),
        out_shape=jax.ShapeDtypeStruct((d, IN_PAD), BF16),
        compiler_params=_params(("arbitrary",)),
        name="pack_in_proj",
    )(w_in)


def _per_head_lanes(w, width):
    k = w.shape[0]
    w = w.reshape(k, HEADS, width)
    return jnp.pad(w, ((0, 0), (0, 0), (0, LANES - width))).reshape(k, HEADS * LANES)


def _row(v):
    return v.reshape(1, -1).astype(F32)


def _head_row(v):
    return jnp.repeat(v.astype(F32), HEAD_W).reshape(1, GROUP_W)


def _block_diag(w):
    nb, n, _ = w.shape
    eye = jnp.eye(nb, dtype=w.dtype)
    return jnp.einsum("gij,gh->gihj", w, eye).reshape(nb * n, nb * n)


def _rope_rows():
    inv_m = ROPE_THETA ** (-jnp.arange(0, MLA_ROPE, 2, dtype=F32) / MLA_ROPE)
    inv_r = ROPE_THETA ** (-jnp.arange(0, HEAD_W, 2, dtype=F32) / HEAD_W)
    zeros = lambda n: jnp.zeros((n,), F32)
    ones = lambda n: jnp.ones((n,), F32)
    half = HEAD_W // 2
    fm = jnp.concatenate([zeros(MLA_NOPE), inv_m, inv_m, zeros(HEAD_W - MLA_QK)])
    sm = jnp.concatenate([zeros(MLA_NOPE), -ones(MLA_ROPE // 2), ones(MLA_ROPE // 2), zeros(HEAD_W - MLA_QK)])
    freq = jnp.concatenate([inv_r, inv_r, fm])
    sign = jnp.concatenate([-ones(half), ones(half), sm])
    return [a.reshape(1, LANES) for a in (freq, sign)]


def _forward(x, positions, p, tiles):
    b, s, d = x.shape
    t = b * s
    depth = p["w_in"].shape[0]
    alpha = (2.0 * depth) ** 0.25
    xf = x.reshape(t, d)
    pos = positions.reshape(t, 1).astype(F32)
    cos_m, sin_m, cos_r, sin_r = _rope_tables(pos, *_rope_rows(), tiles["rope"])

    e_mat = np.zeros((LANES, HEADS * LANES), np.float32)
    vone = np.zeros((1, HEADS * LANES), np.float32)
    partner = np.arange(HEADS * LANES)
    in_span = np.zeros((HEADS * LANES,), np.float32)
    half = MLA_ROPE // 2
    for h in range(HEADS):
        lo = LANES * h + MLA_NOPE
        for j in range(MLA_ROPE):
            e_mat[j, lo + j] = 1.0
        partner[lo:lo + half] = np.arange(lo + half, lo + MLA_ROPE)
        partner[lo + half:lo + MLA_ROPE] = np.arange(lo, lo + half)
        in_span[lo:lo + MLA_ROPE] = 1.0
        vone[0, LANES * h + HEAD_W] = 1.0
    e_sw = jnp.asarray(e_mat[:, partner] * in_span, BF16)
    e_mat = jnp.asarray(e_mat, BF16)
    vone = jnp.asarray(vone, F32)

    for l in range(depth):
        mla_in, zx, dt, ret_in, lru_in = _in_proj(xf, _pack_in_proj(p["w_in"], l), tiles["proj"])

        w_ukv = p["mla_w_ukv"][l].reshape(MLA_KV_LORA, HEADS, MLA_NOPE + HEAD_W)
        wq = _per_head_lanes(p["mla_w_uq"][l], MLA_QK).astype(BF16)
        wq_sw = wq[:, partner] * jnp.asarray(in_span, BF16)
        wk = _per_head_lanes(w_ukv[:, :, :MLA_NOPE].reshape(MLA_KV_LORA, -1), MLA_NOPE).astype(BF16)
        wv = _per_head_lanes(w_ukv[:, :, MLA_NOPE:].reshape(MLA_KV_LORA, -1), HEAD_W).astype(BF16)
        q, k, v = _mla_prep(mla_in, cos_m, sin_m, _row(p["mla_g_q"][l]), _row(p["mla_g_kv"][l]),
                            wq, wq_sw, wk, wv, e_mat, e_sw, vone, b, s, tiles["attn"])
        y_a = _attention(q, k, v, tiles["attn"])

        y_b = _ssd(zx, dt, p["ssd_conv_w"][l].astype(F32), _row(p["ssd_conv_b"][l]),
                   _head_row(p["ssd_dt_bias"][l]), _head_row(p["ssd_a_log"][l]), _head_row(p["ssd_d"][l]),
                   _row(p["ssd_norm_g"][l]), b, s, tiles["ssd"])
        y_c = _retention(ret_in, cos_r, sin_r, _row(p["ret_gn_g"][l]), _row(p["ret_gn_b"][l]),
                         b, s, tiles["ret"])
        y_d = _rglru(lru_in, p["lru_conv_w"][l].astype(F32), _row(p["lru_conv_b"][l]),
                     _block_diag(p["lru_w_a"][l]).astype(BF16), _row(p["lru_b_a"][l]),
                     _block_diag(p["lru_w_x"][l]).astype(BF16), _row(p["lru_b_x"][l]),
                     _row(p["lru_a_param"][l]), b, s, tiles["lru"])

        xf = _mix_ffn((y_a, y_b, y_c, y_d), xf, p["w_out"][l].astype(BF16), _row(p["ln1_g"][l]),
                      _row(p["ln1_b"][l]), p["w_ffn_in"][l].astype(BF16), p["w_ffn_out"][l].astype(BF16),
                      _row(p["ln2_g"][l]), _row(p["ln2_b"][l]), alpha, tiles["ffn"])
    return xf.reshape(b, s, d)


def _tiles(s):
    return {"rope": min(1024, s), "proj": min(1024, s), "ffn": min(512, s), "attn": min(1024, s),
            "ssd": min(2048, s), "ret": min(2048, s), "lru": min(1024, s)}


def kernel(x, positions, w_in, mla_g_q, mla_w_uq, mla_g_kv, mla_w_ukv, ssd_conv_w, ssd_conv_b, ssd_dt_bias, ssd_a_log, ssd_d, ssd_norm_g, ret_gn_g, ret_gn_b, lru_conv_w, lru_conv_b, lru_w_a, lru_b_a, lru_w_x, lru_b_x, lru_a_param, w_out, ln1_g, ln1_b, w_ffn_in, w_ffn_out, ln2_g, ln2_b):
    p = dict(w_in=w_in, mla_g_q=mla_g_q, mla_w_uq=mla_w_uq, mla_g_kv=mla_g_kv, mla_w_ukv=mla_w_ukv,
             ssd_conv_w=ssd_conv_w, ssd_conv_b=ssd_conv_b, ssd_dt_bias=ssd_dt_bias, ssd_a_log=ssd_a_log,
             ssd_d=ssd_d, ssd_norm_g=ssd_norm_g, ret_gn_g=ret_gn_g, ret_gn_b=ret_gn_b,
             lru_conv_w=lru_conv_w, lru_conv_b=lru_conv_b, lru_w_a=lru_w_a, lru_b_a=lru_b_a,
             lru_w_x=lru_w_x, lru_b_x=lru_b_x, lru_a_param=lru_a_param, w_out=w_out, ln1_g=ln1_g,
             ln1_b=ln1_b, w_ffn_in=w_ffn_in, w_ffn_out=w_ffn_out, ln2_g=ln2_g, ln2_b=ln2_b)
    return _forward(x, positions, p, _tiles(x.shape[1]))
```
